```python
import math
import numpy as np
import jax
import jax.numpy as jnp
from jax import lax

D_MODEL = 4096
BATCH = 1
SEQ = 16384
DEPTH = 4

CTX_LEN = 256
GRID_W = 64
EPS = 1e-6

CONV_WIDTH = D_MODEL // 4
CONV_K = 3
S5_WIDTH = D_MODEL // 4
S5_GROUP = 16
S5_GROUPS = S5_WIDTH // S5_GROUP
S5_STATE = 64
POOL_WIDTH = D_MODEL // 4
POOL_WINDOWS = (2, 4, 8, 16)
N_POOL = 4
POOL_GROUP = POOL_WIDTH // N_POOL
HEAD_DIM = 128
N_HEADS = (3 * D_MODEL) // (8 * HEAD_DIM)
GQA_GROUP = 3
N_KV_HEADS = N_HEADS // GQA_GROUP
Q_WIDTH = N_HEADS * HEAD_DIM
KV_WIDTH = N_KV_HEADS * HEAD_DIM
Q_BLOCK = 128
ROPE_THETA = 10000.0
ROPE_PAIRS_AXIS = HEAD_DIM // 4
N_BRANCH = 4
GATE_RANK = 512
ADA_RANK = 256
N_MOD = 6
FFN_HIDDEN = -((-8 * D_MODEL) // (3 * 256)) * 256
IN_SPLITS = (CONV_WIDTH, CONV_WIDTH, CONV_WIDTH, S5_WIDTH, POOL_WIDTH, Q_WIDTH, KV_WIDTH, KV_WIDTH, GATE_RANK)
IN_WIDTH = 3 * CONV_WIDTH + S5_WIDTH + POOL_WIDTH + Q_WIDTH + 2 * KV_WIDTH + GATE_RANK

kernel_name = 'hybrid_parallel_gated_diffusion_trunk'


def rms_norm(x, gain):
    xf = x.astype(jnp.float32)
    y = xf * lax.rsqrt(jnp.mean(xf * xf, axis=-1, keepdims=True) + EPS)
    return (y * gain.astype(jnp.float32)).astype(x.dtype)


def modulate(x, gain, shift, scale):
    return rms_norm(x, gain) * (1 + scale) + shift


def ada_modulation(cond, down, up, bias):
    m = (jax.nn.silu(cond) @ down) @ up + bias
    m = m.reshape(cond.shape[0], N_MOD, 1, D_MODEL)
    return (m[:, 0], m[:, 1], m[:, 2], m[:, 3], m[:, 4], m[:, 5])


def in_projection(h, w_in):
    offsets = np.cumsum(np.array(IN_SPLITS))[:-1].tolist()
    return jnp.split(h @ w_in, offsets, axis=-1)


def rope_tables(length):
    rows = length // GRID_W
    row = jnp.repeat(jnp.arange(rows), GRID_W).astype(jnp.float32)
    col = jnp.tile(jnp.arange(GRID_W), rows).astype(jnp.float32)
    inv = ROPE_THETA ** (-jnp.arange(ROPE_PAIRS_AXIS, dtype=jnp.float32) / ROPE_PAIRS_AXIS)
    ang = jnp.concatenate([row[:, None] * inv, col[:, None] * inv], axis=-1)
    return jnp.cos(ang), jnp.sin(ang)


def apply_rope(x, cos, sin):
    b, l, h, d = x.shape
    xf = x.astype(jnp.float32).reshape(b, l, h, d // 2, 2)
    x0, x1 = xf[..., 0], xf[..., 1]
    cs, sn = cos[None, :, None, :], sin[None, :, None, :]
    out = jnp.stack([x0 * cs - x1 * sn, x0 * sn + x1 * cs], axis=-1)
    return out.reshape(b, l, h, d).astype(x.dtype)


def attention_heads(q, k, v, q_gain, k_gain, rope):
    b, l = q.shape[:2]
    q = rms_norm(q.reshape(b, l, N_HEADS, HEAD_DIM), q_gain)
    k = rms_norm(k.reshape(b, l, N_KV_HEADS, HEAD_DIM), k_gain)
    if rope is not None:
        q = apply_rope(q, rope[0], rope[1])
        k = apply_rope(k, rope[0], rope[1])
    return (q.reshape(b, l, N_KV_HEADS, GQA_GROUP, HEAD_DIM), k,
            v.reshape(b, l, N_KV_HEADS, HEAD_DIM))


def block_attention(q, k, v):
    b, lq = q.shape[:2]
    nblk = lq // Q_BLOCK
    qb = jnp.moveaxis(q.reshape(b, nblk, Q_BLOCK, N_KV_HEADS, GQA_GROUP, HEAD_DIM), 1, 0)
    scale = HEAD_DIM ** -0.5

    def one_block(qblk):
        s = jnp.einsum('bqhgd,bkhd->bhgqk', qblk, k).astype(jnp.float32) * scale
        p = jax.nn.softmax(s, axis=-1).astype(v.dtype)
        return jnp.einsum('bhgqk,bkhd->bqhgd', p, v)

    o = lax.map(one_block, qb)
    return jnp.moveaxis(o, 0, 1).reshape(b, lq, N_HEADS * HEAD_DIM)


def short_conv(a_h, a_b, a_c, conv_w):
    xc = a_c * a_h
    xp = jnp.pad(xc, ((0, 0), (1, 1), (0, 0)))
    conv = conv_w[0] * xp[:, :-2] + conv_w[1] * xp[:, 1:-1] + conv_w[2] * xp[:, 2:]
    return a_b * conv


def multiscale_pool(p_u, pool_w, pool_scale):
    b, l, _ = p_u.shape
    xf = p_u.astype(jnp.float32).reshape(b, l, N_POOL, POOL_GROUP)
    csum = jnp.concatenate([jnp.zeros((b, 1, N_POOL, POOL_GROUP), jnp.float32),
                            jnp.cumsum(xf, axis=1)], axis=1)
    t = jnp.arange(l)
    means = []
    for g, win in enumerate(POOL_WINDOWS):
        lo = jnp.clip(t - win // 2, 0, l)
        hi = jnp.clip(t - win // 2 + win, 0, l)
        cg = csum[:, :, g]
        count = (hi - lo).astype(jnp.float32)[None, :, None]
        means.append((cg[:, hi] - cg[:, lo]) / count)
    pooled = (jnp.stack(means, axis=2) - xf).astype(p_u.dtype)
    y = jnp.einsum('blgc,gcd->blgd', pooled, pool_w)
    return y.reshape(b, l, POOL_WIDTH) * pool_scale


def to_complex(re, im):
    return lax.complex(re.astype(jnp.float32), im.astype(jnp.float32))


def _recurrence_combine(left, right):
    a_l, b_l = left
    a_r, b_r = right
    return a_r * a_l, a_r * b_l + b_r


def s5_scan(u, lam_re, lam_im, log_step, b_re, b_im, init, reverse):
    lam = to_complex(lam_re, lam_im)
    delta = jnp.exp(log_step.astype(jnp.float32))[:, None]
    lam_bar = jnp.exp(lam * delta)
    b_bar = ((lam_bar - 1.0) / lam)[:, :, None] * to_complex(b_re, b_im)
    bu = jnp.einsum('blgs,gps->blgp', u.astype(jnp.complex64), b_bar)
    if init is not None:
        edge = u.shape[1] - 1 if reverse else 0
        bu = bu.at[:, edge].add(lam_bar * init)
    a = jnp.broadcast_to(lam_bar, bu.shape)
    _, h = lax.associative_scan(_recurrence_combine, (a, bu), reverse=reverse, axis=1)
    return h


def s5_states(s_u, lp, init_f, init_b):
    b, l = s_u.shape[:2]
    u = s_u.astype(jnp.float32).reshape(b, l, S5_GROUPS, S5_GROUP)
    h_f = s5_scan(u, lp['s5_lambda_re'][0], lp['s5_lambda_im'][0], lp['s5_log_step'][0],
                  lp['s5_b_re'][0], lp['s5_b_im'][0], init_f, False)
    h_b = s5_scan(u, lp['s5_lambda_re'][1], lp['s5_lambda_im'][1], lp['s5_log_step'][1],
                  lp['s5_b_re'][1], lp['s5_b_im'][1], init_b, True)
    return h_f, h_b


def s5_readout(h, c_re, c_im):
    return jnp.real(jnp.einsum('blgp,gsp->blgs', h, to_complex(c_re, c_im)))


def s5_branch_output(s_u, h_f, h_b, lp):
    b, l = s_u.shape[:2]
    y = (s5_readout(h_f, lp['s5_c_re'][0], lp['s5_c_im'][0])
         + s5_readout(h_b, lp['s5_c_re'][1], lp['s5_c_im'][1])).reshape(b, l, S5_WIDTH)
    y = jax.nn.gelu(y + lp['s5_d'].astype(jnp.float32) * s_u.astype(jnp.float32)).astype(s_u.dtype)
    return y * jax.nn.sigmoid(y @ lp['s5_w_glu'])


def token_mixers(parts, q, k_all, v_all, h_f, h_b, lp):
    a_h, a_b, a_c, s_u, p_u = parts[0], parts[1], parts[2], parts[3], parts[4]
    g_low = parts[8]
    y_conv = short_conv(a_h, a_b, a_c, lp['conv_w'])
    y_s5 = s5_branch_output(s_u, h_f, h_b, lp)
    y_pool = multiscale_pool(p_u, lp['pool_w'], lp['pool_scale'])
    y_attn = block_attention(q, k_all, v_all)

    def gated(i, y, w_out):
        gate = jax.nn.sigmoid(g_low @ lp['gate_up'][i] + lp['gate_bias'][i])
        return gate * (y @ w_out)

    merged = (gated(0, y_conv, lp['w_out_conv']) + gated(1, y_s5, lp['w_out_s5'])
              + gated(2, y_pool, lp['w_out_pool']) + gated(3, y_attn, lp['w_out_attn']))
    return merged @ lp['w_o']


def swiglu(h, w_gate, w_up, w_down):
    return (jax.nn.silu(h @ w_gate) * (h @ w_up)) @ w_down


def trunk_layer(x_lat, x_ctx, mod_lat, mod_ctx, lp, rope, last):
    sh_m, sc_m, gt_m, sh_f, sc_f, gt_f = mod_lat
    csh_m, csc_m, cgt_m, csh_f, csc_f, cgt_f = mod_ctx
    g_pre_m, g_post_m = lp['norm_gains'][0], lp['norm_gains'][1]
    g_pre_f, g_post_f = lp['norm_gains'][2], lp['norm_gains'][3]
    q_gain, k_gain = lp['qk_norm'][0], lp['qk_norm'][1]

    c_parts = in_projection(modulate(x_ctx, g_pre_m, csh_m, csc_m), lp['w_in'])
    cq, ck, cv = attention_heads(c_parts[5], c_parts[6], c_parts[7], q_gain, k_gain, None)
    ch_f, ch_b = s5_states(c_parts[3], lp, None, None)

    l_parts = in_projection(modulate(x_lat, g_pre_m, sh_m, sc_m), lp['w_in'])
    q, k, v = attention_heads(l_parts[5], l_parts[6], l_parts[7], q_gain, k_gain, rope)
    k_all = jnp.concatenate([ck, k], axis=1)
    v_all = jnp.concatenate([cv, v], axis=1)
    h_f, h_b = s5_states(l_parts[3], lp, ch_f[:, -1], ch_b[:, 0])
    mix = token_mixers(l_parts, q, k_all, v_all, h_f, h_b, lp)
    x_lat = x_lat + gt_m * rms_norm(mix, g_post_m)
    ffn = swiglu(modulate(x_lat, g_pre_f, sh_f, sc_f), lp['ffn_w_gate'], lp['ffn_w_up'], lp['ffn_w_down'])
    x_lat = x_lat + gt_f * rms_norm(ffn, g_post_f)

    if not last:
        cmix = token_mixers(c_parts, cq, ck, cv, ch_f, ch_b, lp)
        x_ctx = x_ctx + cgt_m * rms_norm(cmix, g_post_m)
        cffn = swiglu(modulate(x_ctx, g_pre_f, csh_f, csc_f), lp['ffn_w_gate'], lp['ffn_w_up'], lp['ffn_w_down'])
        x_ctx = x_ctx + cgt_f * rms_norm(cffn, g_post_f)
    return x_lat, x_ctx


def setup_inputs(seed: int = 0) -> dict:
    key = jax.random.key(seed)
    keys = iter(jax.random.split(key, 40))

    def normal(shape, scale):
        return jax.random.normal(next(keys), shape, jnp.float32) * scale

    def gain(shape):
        return 1.0 + normal(shape, 0.02)

    L = DEPTH
    s5_shape = (L, 2, S5_GROUPS, S5_STATE)
    lam_im = jnp.pi * jnp.arange(S5_STATE, dtype=jnp.float32) + normal(s5_shape, 0.01)
    return {
        'x': normal((BATCH, SEQ, D_MODEL), 1.0),
        'c': normal((BATCH, D_MODEL), 1.0),
        'ctx': normal((BATCH, CTX_LEN, D_MODEL), 1.0),
        'c_ctx': normal((D_MODEL,), 1.0),
        'ada_down': normal((L, D_MODEL, ADA_RANK), D_MODEL ** -0.5),
        'ada_up': normal((L, ADA_RANK, N_MOD * D_MODEL), 0.5 * ADA_RANK ** -0.5),
        'ada_bias': normal((L, N_MOD * D_MODEL), 0.02),
        'norm_gains': gain((L, 4, D_MODEL)),
        'w_in': normal((L, D_MODEL, IN_WIDTH), D_MODEL ** -0.5),
        'conv_w': normal((L, CONV_K, CONV_WIDTH), CONV_K ** -0.5),
        's5_lambda_re': -0.5 + normal(s5_shape, 0.01),
        's5_lambda_im': lam_im,
        's5_log_step': jax.random.uniform(next(keys), (L, 2, S5_GROUPS), dtype=jnp.float32,
                                          minval=math.log(1e-3), maxval=math.log(1e-1)),
        's5_b_re': normal((L, 2, S5_GROUPS, S5_STATE, S5_GROUP), (2 * S5_GROUP) ** -0.5),
        's5_b_im': normal((L, 2, S5_GROUPS, S5_STATE, S5_GROUP), (2 * S5_GROUP) ** -0.5),
        's5_c_re': normal((L, 2, S5_GROUPS, S5_GROUP, S5_STATE), 0.5 ** 0.5),
        's5_c_im': normal((L, 2, S5_GROUPS, S5_GROUP, S5_STATE), 0.5 ** 0.5),
        's5_d': normal((L, S5_WIDTH), 1.0),
        's5_w_glu': normal((L, S5_WIDTH, S5_WIDTH), S5_WIDTH ** -0.5),
        'pool_w': normal((L, N_POOL, POOL_GROUP, POOL_GROUP), POOL_GROUP ** -0.5),
        'pool_scale': 1.0 + normal((L, POOL_WIDTH), 0.1),
        'qk_norm': gain((L, 2, HEAD_DIM)),
        'gate_up': normal((L, N_BRANCH, GATE_RANK, D_MODEL), GATE_RANK ** -0.5),
        'gate_bias': normal((L, N_BRANCH, D_MODEL), 0.02),
        'w_out_conv': normal((L, CONV_WIDTH, D_MODEL), CONV_WIDTH ** -0.5),
        'w_out_s5': normal((L, S5_WIDTH, D_MODEL), S5_WIDTH ** -0.5),
        'w_out_pool': normal((L, POOL_WIDTH, D_MODEL), POOL_WIDTH ** -0.5),
        'w_out_attn': normal((L, Q_WIDTH, D_MODEL), Q_WIDTH ** -0.5),
        'w_o': normal((L, D_MODEL, D_MODEL), D_MODEL ** -0.5),
        'ffn_w_gate': normal((L, D_MODEL, FFN_HIDDEN), D_MODEL ** -0.5),
        'ffn_w_up': normal((L, D_MODEL, FFN_HIDDEN), D_MODEL ** -0.5),
        'ffn_w_down': normal((L, FFN_HIDDEN, D_MODEL), FFN_HIDDEN ** -0.5),
    }


def reference(x, c, ctx, c_ctx, ada_down, ada_up, ada_bias, norm_gains, w_in, conv_w,
              s5_lambda_re, s5_lambda_im, s5_log_step, s5_b_re, s5_b_im, s5_c_re, s5_c_im,
              s5_d, s5_w_glu, pool_w, pool_scale, qk_norm, gate_up, gate_bias,
              w_out_conv, w_out_s5, w_out_pool, w_out_attn, w_o, ffn_w_gate, ffn_w_up, ffn_w_down):
    rope = rope_tables(x.shape[1])
    h_lat, h_ctx = x, ctx
    for i in range(DEPTH):
        lp = {
            'norm_gains': norm_gains[i], 'w_in': w_in[i], 'conv_w': conv_w[i],
            's5_lambda_re': s5_lambda_re[i], 's5_lambda_im': s5_lambda_im[i],
            's5_log_step': s5_log_step[i], 's5_b_re': s5_b_re[i], 's5_b_im': s5_b_im[i],
            's5_c_re': s5_c_re[i], 's5_c_im': s5_c_im[i], 's5_d': s5_d[i], 's5_w_glu': s5_w_glu[i],
            'pool_w': pool_w[i], 'pool_scale': pool_scale[i], 'qk_norm': qk_norm[i],
            'gate_up': gate_up[i], 'gate_bias': gate_bias[i],
            'w_out_conv': w_out_conv[i], 'w_out_s5': w_out_s5[i], 'w_out_pool': w_out_pool[i],
            'w_out_attn': w_out_attn[i], 'w_o': w_o[i],
            'ffn_w_gate': ffn_w_gate[i], 'ffn_w_up': ffn_w_up[i], 'ffn_w_down': ffn_w_down[i],
        }
        mod_lat = ada_modulation(c, ada_down[i], ada_up[i], ada_bias[i])
        mod_ctx = ada_modulation(c_ctx[None, :], ada_down[i], ada_up[i], ada_bias[i])
        h_lat, h_ctx = trunk_layer(h_lat, h_ctx, mod_lat, mod_ctx, lp, rope, i == DEPTH - 1)
    return h_lat
```

```python
import functools

import jax
import jax.numpy as jnp
import numpy as np
from jax import lax
from jax.experimental import pallas as pl
from jax.experimental.pallas import tpu as pltpu

EPS = 1e-6
GRID_W = 64
ROPE_THETA = 10000.0
POOL_WINDOWS = (2, 4, 8, 16)
N_MOD = 6
S5_CHUNK = 16
HALO = 8
NEG_BIG = -1e30
V7X_VMEM_LIMIT = 56 * 1024 * 1024
F32 = jnp.float32
BF16 = jnp.bfloat16
HIGHEST = lax.Precision.HIGHEST


def _params(*sem):
    return pltpu.CompilerParams(dimension_semantics=sem, vmem_limit_bytes=V7X_VMEM_LIMIT)


def _tile(n, target, mult):
    best = None
    for t in range(mult, min(n, target) + 1, mult):
        if n % t == 0:
            best = t
    assert best is not None, (n, target, mult)
    return best


def _rms(x, gain):
    return x * lax.rsqrt(jnp.mean(x * x, axis=-1, keepdims=True) + EPS) * gain


def _sigmoid(x):
    return 1.0 / (1.0 + jnp.exp(-x))


def _ada_body(cond_ref, down_ref, up_ref, bias_ref, o_ref):
    s = cond_ref[...]
    s = s * _sigmoid(s)
    low = jnp.dot(s, down_ref[0], precision=HIGHEST, preferred_element_type=F32)
    o_ref[0] = jnp.dot(low, up_ref[0], precision=HIGHEST, preferred_element_type=F32) + bias_ref[0]


def ada_modulation(cond, down, up, bias):
    depth, d, rank = down.shape
    width = up.shape[-1]
    tn = _tile(width, 4096, 128)
    return pl.pallas_call(
        _ada_body,
        grid=(depth, width // tn),
        in_specs=[
            pl.BlockSpec((8, d), lambda l, j: (0, 0)),
            pl.BlockSpec((1, d, rank), lambda l, j: (l, 0, 0)),
            pl.BlockSpec((1, rank, tn), lambda l, j: (l, 0, j)),
            pl.BlockSpec((1, 1, tn), lambda l, j: (l, 0, j)),
        ],
        out_specs=pl.BlockSpec((1, 8, tn), lambda l, j: (l, 0, j)),
        out_shape=jax.ShapeDtypeStruct((depth, 8, width), F32),
        compiler_params=_params("arbitrary", "arbitrary"),
        name="ada_modulation",
    )(cond, down, up, bias.reshape(depth, 1, width))


def _modulate_body(x_ref, mod_ref, gain_ref, h_ref, *, shift_row, scale_row, gain_row):
    m = mod_ref[0]
    h = _rms(x_ref[...], gain_ref[gain_row:gain_row + 1])
    h = h * (1.0 + m[scale_row:scale_row + 1]) + m[shift_row:shift_row + 1]
    h_ref[...] = h.astype(h_ref.dtype)


def modulate(x, mod, gains, *, ctx_tiles, tr, shift_row, scale_row, gain_row):
    t, d = x.shape
    who = lambda i: (jnp.where(i < ctx_tiles, 0, 1), 0, 0)
    return pl.pallas_call(
        functools.partial(_modulate_body, shift_row=shift_row, scale_row=scale_row, gain_row=gain_row),
        grid=(t // tr,),
        in_specs=[
            pl.BlockSpec((tr, d), lambda i: (i, 0)),
            pl.BlockSpec((1, 8, d), who),
            pl.BlockSpec(gains.shape, lambda i: (0, 0)),
        ],
        out_specs=pl.BlockSpec((tr, d), lambda i: (i, 0)),
        out_shape=jax.ShapeDtypeStruct((t, d), BF16),
        compiler_params=_params("parallel"),
        name="modulate",
    )(x, mod, gains)


def _residual_body(*refs, gate_row, post_row, nxt):
    if nxt is None:
        x_ref, y_ref, mod_ref, gain_ref, xo_ref = refs
    else:
        x_ref, y_ref, mod_ref, gain_ref, modn_ref, gainn_ref, xo_ref, h_ref = refs
    m = mod_ref[0]
    xn = x_ref[...] + m[gate_row:gate_row + 1] * _rms(y_ref[...], gain_ref[post_row:post_row + 1])
    xo_ref[...] = xn
    if nxt is not None:
        shift_row, scale_row, gain_row = nxt
        mn = modn_ref[0]
        h = _rms(xn, gainn_ref[gain_row:gain_row + 1])
        h = h * (1.0 + mn[scale_row:scale_row + 1]) + mn[shift_row:shift_row + 1]
        h_ref[...] = h.astype(h_ref.dtype)


def residual(x, y, mod, gains, *, ctx_tiles, tr, gate_row, post_row, mod_next=None, gains_next=None,
             nxt=None):
    t, d = x.shape
    who = lambda i: (jnp.where(i < ctx_tiles, 0, 1), 0, 0)
    row = pl.BlockSpec((tr, d), lambda i: (i, 0))
    in_specs = [row, row, pl.BlockSpec((1, 8, d), who), pl.BlockSpec(gains.shape, lambda i: (0, 0))]
    args = [x, y, mod, gains]
    out_specs = [row]
    out_shape = [jax.ShapeDtypeStruct((t, d), F32)]
    if nxt is not None:
        in_specs += [pl.BlockSpec((1, 8, d), who), pl.BlockSpec(gains_next.shape, lambda i: (0, 0))]
        args += [mod_next, gains_next]
        out_specs.append(row)
        out_shape.append(jax.ShapeDtypeStruct((t, d), BF16))
    out = pl.pallas_call(
        functools.partial(_residual_body, gate_row=gate_row, post_row=post_row, nxt=nxt),
        grid=(t // tr,),
        in_specs=in_specs,
        out_specs=out_specs,
        out_shape=out_shape,
        compiler_params=_params("parallel"),
        name="residual_norm",
    )(*args)
    return out if nxt is not None else (out[0], None)


def _matmul_body(a_ref, b_ref, o_ref, *, nk):
    r = jnp.dot(a_ref[...], b_ref[...], preferred_element_type=F32)
    if nk == 1:
        o_ref[...] = r.astype(o_ref.dtype)
    else:
        k = pl.program_id(2)

        @pl.when(k == 0)
        def _():
            o_ref[...] = r

        @pl.when(k > 0)
        def _():
            o_ref[...] += r


def matmul(a, b, *, tm, tn, nk=1, out_dtype=F32):
    m, k = a.shape
    n = b.shape[1]
    assert k % nk == 0 and (nk == 1 or out_dtype == F32)
    tk = k // nk
    return pl.pallas_call(
        functools.partial(_matmul_body, nk=nk),
        grid=(m // tm, n // tn, nk),
        in_specs=[
            pl.BlockSpec((tm, tk), lambda i, j, kk: (i, kk)),
            pl.BlockSpec((tk, tn), lambda i, j, kk: (kk, j)),
        ],
        out_specs=pl.BlockSpec((tm, tn), lambda i, j, kk: (i, j)),
        out_shape=jax.ShapeDtypeStruct((m, n), out_dtype),
        compiler_params=_params("parallel", "parallel", "arbitrary"),
        name="matmul",
    )(a, b)


def _ffn_up_body(a_ref, w_ref, o_ref, *, tn):
    r = jnp.dot(a_ref[...], w_ref[...], preferred_element_type=F32)
    g, u = r[:, :tn], r[:, tn:]
    o_ref[...] = (g * _sigmoid(g) * u).astype(o_ref.dtype)


def ffn_up(a, w_gu, *, tm, tn):
    m, k = a.shape
    n = w_gu.shape[1] // 2
    return pl.pallas_call(
        functools.partial(_ffn_up_body, tn=tn),
        grid=(m // tm, n // tn),
        in_specs=[
            pl.BlockSpec((tm, k), lambda i, j: (i, 0)),
            pl.BlockSpec((k, 2 * tn), lambda i, j: (0, j)),
        ],
        out_specs=pl.BlockSpec((tm, tn), lambda i, j: (i, j)),
        out_shape=jax.ShapeDtypeStruct((m, n), BF16),
        compiler_params=_params("parallel", "parallel"),
        name="ffn_up",
    )(a, w_gu)


def _qk_prep_body(x_ref, gain_ref, cos_ref, sin_ref, o_ref, *, scale, transpose):
    y = _rms(x_ref[...], gain_ref[...])
    y = y * cos_ref[...] + pltpu.roll(y, y.shape[-1] // 2, 1) * sin_ref[...]
    if scale != 1.0:
        y = y * scale
    if transpose:
        o_ref[0] = y.T.astype(o_ref.dtype)
    else:
        o_ref[0] = y.astype(o_ref.dtype)


def qk_prep(p, gain, cos, sin, *, col0, heads, tr, scale, transpose):
    t = p.shape[0]
    dh = gain.shape[-1]
    if transpose:
        out_spec = pl.BlockSpec((1, dh, tr), lambda i, h: (h, 0, i))
        out_shape = jax.ShapeDtypeStruct((heads, dh, t), BF16)
    else:
        out_spec = pl.BlockSpec((1, tr, dh), lambda i, h: (h, i, 0))
        out_shape = jax.ShapeDtypeStruct((heads, t, dh), BF16)
    return pl.pallas_call(
        functools.partial(_qk_prep_body, scale=scale, transpose=transpose),
        grid=(t // tr, heads),
        in_specs=[
            pl.BlockSpec((tr, dh), lambda i, h: (i, col0 + h)),
            pl.BlockSpec((1, dh), lambda i, h: (0, 0)),
            pl.BlockSpec((tr, dh), lambda i, h: (i, 0)),
            pl.BlockSpec((tr, dh), lambda i, h: (i, 0)),
        ],
        out_specs=out_spec,
        out_shape=out_shape,
        compiler_params=_params("parallel", "parallel"),
        name="qk_prep",
    )(p, gain, cos, sin)


def _v_prep_body(x_ref, o_ref):
    o_ref[0, 0] = x_ref[...].T.astype(o_ref.dtype)


def v_prep(p, *, col0, heads, dh, tk):
    t = p.shape[0]
    return pl.pallas_call(
        _v_prep_body,
        grid=(t // tk, heads),
        in_specs=[pl.BlockSpec((tk, dh), lambda i, h: (i, col0 + h))],
        out_specs=pl.BlockSpec((1, 1, dh, tk), lambda i, h: (h, i, 0, 0)),
        out_shape=jax.ShapeDtypeStruct((heads, t // tk, dh, tk), BF16),
        compiler_params=_params("parallel", "parallel"),
        name="v_prep",
    )(p)


def _flash_body(q_ref, k_ref, v_ref, o_ref, *, tq, tk, group, ctx_len, ctx_q_tiles, n_chunks):
    dh = q_ref.shape[1]
    m_cols = group * tq
    q_t = jnp.concatenate([q_ref[g] for g in range(group)], axis=1)

    def step(c, carry, masked):
        m, l, acc = carry
        k_c = k_ref[0, pl.ds(pl.multiple_of(c * tk, tk), tk), :]
        s = jnp.dot(k_c, q_t, preferred_element_type=F32)
        if masked:
            key = c * tk + lax.broadcasted_iota(jnp.int32, (tk, 1), 0)
            s = jnp.where(key < ctx_len, s, NEG_BIG)
        m_new = jnp.maximum(m, jnp.max(s, axis=0, keepdims=True))
        alpha = jnp.exp(m - m_new)
        p = jnp.exp(s - m_new)
        l = alpha * l + jnp.sum(p, axis=0, keepdims=True)
        acc = alpha * acc + jnp.dot(v_ref[0, c], p.astype(BF16), preferred_element_type=F32)
        return m_new, l, acc

    def run(chunks, masked):
        init = (jnp.full((1, m_cols), NEG_BIG, F32), jnp.zeros((1, m_cols), F32),
                jnp.zeros((dh, m_cols), F32))
        _, l, acc = lax.fori_loop(0, chunks, functools.partial(step, masked=masked), init)
        o = acc / l
        for g in range(group):
            o_ref[:, g * dh:(g + 1) * dh] = o[:, g * tq:(g + 1) * tq].T.astype(o_ref.dtype)

    i = pl.program_id(1)

    @pl.when(i < ctx_q_tiles)
    def _():
        run(-(-ctx_len // tk), True)

    @pl.when(i >= ctx_q_tiles)
    def _():
        run(n_chunks, False)


def flash_attention(q_t, k, v_t, *, tq, ctx_len):
    heads, dh, t = q_t.shape
    kv, n_chunks, _, tk = v_t.shape
    group = heads // kv
    assert ctx_len % tq == 0
    return pl.pallas_call(
        functools.partial(_flash_body, tq=tq, tk=tk, group=group, ctx_len=ctx_len,
                          ctx_q_tiles=ctx_len // tq, n_chunks=n_chunks),
        grid=(kv, t // tq),
        in_specs=[
            pl.BlockSpec((group, dh, tq), lambda h, i: (h, 0, i)),
            pl.BlockSpec((1, t, dh), lambda h, i: (h, 0, 0)),
            pl.BlockSpec((1, n_chunks, dh, tk), lambda h, i: (h, 0, 0, 0)),
        ],
        out_specs=pl.BlockSpec((tq, group * dh), lambda h, i: (i, h)),
        out_shape=jax.ShapeDtypeStruct((t, heads * dh), BF16),
        compiler_params=_params("parallel", "parallel"),
        name="flash_attention",
    )(q_t, k, v_t)


def _convpool_body(ah_ref, ab_ref, ac_ref, pu_ref, ah_up, ac_up, pu_up, ah_dn, ac_dn, pu_dn,
                   cw_ref, pw_ref, ps_ref, yc_ref, yp_ref, xbuf, pbuf,
                   *, tr, ctx_tiles, n_tiles, ctx_len, seq_len):
    i = pl.program_id(0)
    has_up = jnp.logical_and(i != 0, i != ctx_tiles)
    has_dn = jnp.logical_and(i != ctx_tiles - 1, i != n_tiles - 1)

    xbuf[0:HALO] = jnp.where(has_up, ac_up[...] * ah_up[...], 0.0)
    xbuf[HALO:HALO + tr] = ac_ref[...] * ah_ref[...]
    xbuf[HALO + tr:HALO + tr + HALO] = jnp.where(has_dn, ac_dn[...] * ah_dn[...], 0.0)
    conv = (cw_ref[0:1] * xbuf[HALO - 1:HALO - 1 + tr] + cw_ref[1:2] * xbuf[HALO:HALO + tr]
            + cw_ref[2:3] * xbuf[HALO + 1:HALO + 1 + tr])
    yc_ref[...] = (ab_ref[...] * conv).astype(yc_ref.dtype)

    pbuf[0:HALO] = jnp.where(has_up, pu_up[...], 0.0)
    pbuf[HALO:HALO + tr] = pu_ref[...]
    pbuf[HALO + tr:HALO + tr + HALO] = jnp.where(has_dn, pu_dn[...], 0.0)
    in_ctx = i < ctx_tiles
    pos = lax.broadcasted_iota(jnp.int32, (tr, 1), 0) + jnp.where(in_ctx, i, i - ctx_tiles) * tr
    length = jnp.where(in_ctx, ctx_len, seq_len)
    pg = pw_ref.shape[-1]
    for g, win in enumerate(POOL_WINDOWS):
        cols = slice(g * pg, (g + 1) * pg)
        total = None
        for s in range(-(win // 2), win - win // 2):
            piece = pbuf[HALO + s:HALO + s + tr, cols]
            total = piece if total is None else total + piece
        count = jnp.minimum(pos - win // 2 + win, length) - jnp.maximum(pos - win // 2, 0)
        pooled = total / count.astype(F32) - pu_ref[:, cols]
        y = jnp.dot(pooled.astype(BF16), pw_ref[g], preferred_element_type=F32) * ps_ref[:, cols]
        yp_ref[:, cols] = y.astype(yp_ref.dtype)


def conv_pool(p, conv_w, pool_w, pool_scale, *, width, tr, ctx_len, seq_len):
    t = p.shape[0]
    n_tiles = t // tr
    ctx_tiles = ctx_len // tr
    assert ctx_len % tr == 0 and tr % HALO == 0 and max(POOL_WINDOWS) // 2 <= HALO
    hb = tr // HALO
    main = lambda c: pl.BlockSpec((tr, width), lambda i: (i, c))
    up = lambda c: pl.BlockSpec((HALO, width), lambda i: (jnp.maximum(i * hb - 1, 0), c))
    dn = lambda c: pl.BlockSpec((HALO, width), lambda i: (jnp.minimum((i + 1) * hb, t // HALO - 1), c))
    return pl.pallas_call(
        functools.partial(_convpool_body, tr=tr, ctx_tiles=ctx_tiles, n_tiles=n_tiles,
                          ctx_len=ctx_len, seq_len=seq_len),
        grid=(n_tiles,),
        in_specs=[main(0), main(1), main(2), main(4), up(0), up(2), up(4), dn(0), dn(2), dn(4),
                  pl.BlockSpec(conv_w.shape, lambda i: (0, 0)),
                  pl.BlockSpec(pool_w.shape, lambda i: (0, 0, 0)),
                  pl.BlockSpec((1, width), lambda i: (0, 0))],
        out_specs=[pl.BlockSpec((tr, width), lambda i: (i, 0))] * 2,
        out_shape=[jax.ShapeDtypeStruct((t, width), BF16)] * 2,
        scratch_shapes=[pltpu.VMEM((tr + 2 * HALO, width), F32)] * 2,
        compiler_params=_params("parallel"),
        name="conv_pool",
    )(p, p, p, p, p, p, p, p, p, p, conv_w, pool_w, pool_scale.reshape(1, width))


def s5_chunk_weights(lam_re, lam_im, log_step, b_re, b_im, c_re, c_im):
    tc = S5_CHUNK
    lam = lax.complex(lam_re.astype(F32), lam_im.astype(F32))
    z = lam * jnp.exp(log_step.astype(F32))[..., None]
    lam_bar = jnp.exp(z)
    b_bar = ((lam_bar - 1.0) / lam)[..., None] * lax.complex(b_re.astype(F32), b_im.astype(F32))
    c = lax.complex(c_re.astype(F32), c_im.astype(F32))
    depth, _, groups, p_dim, s_dim = b_bar.shape
    powers = jnp.exp(z[..., None, :] * jnp.arange(tc + 1, dtype=F32)[:, None])

    cp = c[..., None, :, :] * powers[..., :tc, None, :]
    kern = (jnp.einsum('ldgtop,ldgpi->ldgtoi', jnp.real(cp), jnp.real(b_bar), precision=HIGHEST)
            - jnp.einsum('ldgtop,ldgpi->ldgtoi', jnp.imag(cp), jnp.imag(b_bar), precision=HIGHEST))
    step = np.arange(tc)
    lag = step[None, :] - step[:, None]
    k_f = kern[:, 0][:, :, np.clip(lag, 0, tc - 1)] * (lag >= 0)[:, :, None, None]
    k_b = kern[:, 1][:, :, np.clip(-lag, 0, tc - 1)] * (lag <= 0)[:, :, None, None]
    toeplitz = jnp.transpose(k_f + k_b, (0, 1, 2, 5, 3, 4)).reshape(depth, groups, tc * s_dim, tc * s_dim)

    f_f = c[:, 0][:, :, None] * powers[:, 0, :, 1:tc + 1, None, :]
    f_b = c[:, 1][:, :, None] * powers[:, 1, :, tc:0:-1, None, :]
    rows = lambda a: jnp.transpose(a, (0, 1, 4, 2, 3)).reshape(depth, groups, p_dim, tc * s_dim)
    w_out = jnp.concatenate([toeplitz, rows(jnp.real(f_f)), rows(jnp.real(f_b)),
                             rows(-jnp.imag(f_f)), rows(-jnp.imag(f_b))], axis=2)

    e_f = powers[:, 0, :, tc - 1::-1][:, :, :tc, :, None] * b_bar[:, 0][:, :, None]
    e_b = powers[:, 1, :, :tc, :, None] * b_bar[:, 1][:, :, None]
    cols = lambda a: jnp.transpose(a, (0, 1, 2, 4, 3)).reshape(depth, groups, tc * s_dim, p_dim)
    w_state = jnp.concatenate([cols(jnp.real(e_f)), cols(jnp.real(e_b)),
                               cols(jnp.imag(e_f)), cols(jnp.imag(e_b))], axis=3)

    last = powers[..., tc, :]
    decay = jnp.stack([jnp.concatenate([jnp.real(last[:, 0]), jnp.real(last[:, 1])], axis=-1),
                       jnp.concatenate([jnp.imag(last[:, 0]), jnp.imag(last[:, 1])], axis=-1)], axis=2)
    return w_state.astype(BF16), w_out.astype(BF16), decay


def _s5_body(u_ref, ws_ref, wo_ref, decay_ref, y_ref, s_scr, hf_scr, hb_scr, *, gb, n, ctx_chunks):
    half = decay_ref.shape[-1]
    cs = u_ref.shape[-1]
    is_fwd = lax.broadcasted_iota(jnp.int32, (1, half), 1) < half // 2
    for g in range(gb):
        s_scr[g] = jnp.dot(u_ref[g], ws_ref[g], preferred_element_type=F32)

    def scan(j, carry):
        jb = jnp.where(j < ctx_chunks, ctx_chunks - 1 - j, n - 1 + ctx_chunks - j)
        out = []
        for g in range(gb):
            c_re, c_im = carry[g]
            hf_scr[g, pl.ds(j, 1), :] = jnp.concatenate([c_re, c_im], axis=1)
            hb_scr[g, pl.ds(jb, 1), :] = jnp.concatenate([c_re, c_im], axis=1)
            row_f = s_scr[g, pl.ds(j, 1), :]
            row_b = s_scr[g, pl.ds(jb, 1), :]
            s_re = jnp.where(is_fwd, row_f[:, :half], row_b[:, :half])
            s_im = jnp.where(is_fwd, row_f[:, half:], row_b[:, half:])
            a_re, a_im = decay_ref[g, 0:1, :], decay_ref[g, 1:2, :]
            out.append((a_re * c_re - a_im * c_im + s_re, a_re * c_im + a_im * c_re + s_im))
        return tuple(out)

    zero = jnp.zeros((1, half), F32)
    lax.fori_loop(0, n, scan, tuple((zero, zero) for _ in range(gb)))

    for g in range(gb):
        h_re = jnp.where(is_fwd, hf_scr[g, :, :half], hb_scr[g, :, :half]).astype(BF16)
        h_im = jnp.where(is_fwd, hf_scr[g, :, half:], hb_scr[g, :, half:]).astype(BF16)
        y_ref[g] = (jnp.dot(u_ref[g], wo_ref[g, :cs], preferred_element_type=F32)
                    + jnp.dot(h_re, wo_ref[g, cs:cs + half], preferred_element_type=F32)
                    + jnp.dot(h_im, wo_ref[g, cs + half:], preferred_element_type=F32))


def s5_mix(u_chunks, w_state, w_out, decay, *, ctx_chunks, gb):
    groups, n, cs = u_chunks.shape
    st = w_state.shape[-1]
    return pl.pallas_call(
        functools.partial(_s5_body, gb=gb, n=n, ctx_chunks=ctx_chunks),
        grid=(groups // gb,),
        in_specs=[
            pl.BlockSpec((gb, n, cs), lambda i: (i, 0, 0)),
            pl.BlockSpec((gb, cs, st), lambda i: (i, 0, 0)),
            pl.BlockSpec((gb, cs + st, cs), lambda i: (i, 0, 0)),
            pl.BlockSpec((gb, 2, st // 2), lambda i: (i, 0, 0)),
        ],
        out_specs=pl.BlockSpec((gb, n, cs), lambda i: (i, 0, 0)),
        out_shape=jax.ShapeDtypeStruct((groups, n, cs), F32),
        scratch_shapes=[pltpu.VMEM((gb, n, st), F32)] * 3,
        compiler_params=_params("parallel"),
        name="s5_mix",
    )(u_chunks, w_state, w_out, decay)


def _s5_glu_body(y_ref, u_ref, d_ref, w_ref, o_ref):
    z = jax.nn.gelu(y_ref[...] + d_ref[...] * u_ref[...])
    gate = _sigmoid(jnp.dot(z.astype(BF16), w_ref[...], preferred_element_type=F32))
    o_ref[...] = (z * gate).astype(o_ref.dtype)


def s5_glu(y, p, d, w_glu, *, width, tr):
    t = p.shape[0]
    return pl.pallas_call(
        _s5_glu_body,
        grid=(t // tr,),
        in_specs=[
            pl.BlockSpec((tr, width), lambda i: (i, 0)),
            pl.BlockSpec((tr, width), lambda i: (i, 3)),
            pl.BlockSpec((1, width), lambda i: (0, 0)),
            pl.BlockSpec((width, width), lambda i: (0, 0)),
        ],
        out_specs=pl.BlockSpec((tr, width), lambda i: (i, 0)),
        out_shape=jax.ShapeDtypeStruct((t, width), BF16),
        compiler_params=_params("parallel"),
        name="s5_glu",
    )(y, p, d.reshape(1, width), w_glu)


def _merge_body(gl_ref, yc_ref, ys_ref, yp_ref, ya_ref, gu_ref, gb_ref, wc_ref, ws_ref, wp_ref, wa_ref,
                o_ref):
    g_low = gl_ref[...].astype(BF16)
    total = None
    for b, (y_ref, w_ref) in enumerate(((yc_ref, wc_ref), (ys_ref, ws_ref), (yp_ref, wp_ref),
                                        (ya_ref, wa_ref))):
        gate = _sigmoid(jnp.dot(g_low, gu_ref[b], preferred_element_type=F32) + gb_ref[b])
        term = gate * jnp.dot(y_ref[...], w_ref[...], preferred_element_type=F32)
        total = term if total is None else total + term
    o_ref[...] = total.astype(o_ref.dtype)


def merge(p, ys, gate_up, gate_bias, w_outs, *, gate_col, tm, tn):
    t = p.shape[0]
    nb, rank, d = gate_up.shape
    row = lambda a: pl.BlockSpec((tm, a.shape[1]), lambda i, j: (i, 0))
    col = lambda a: pl.BlockSpec((a.shape[0], tn), lambda i, j: (0, j))
    return pl.pallas_call(
        _merge_body,
        grid=(t // tm, d // tn),
        in_specs=[pl.BlockSpec((tm, rank), lambda i, j: (i, gate_col))] + [row(y) for y in ys]
                 + [pl.BlockSpec((nb, rank, tn), lambda i, j: (0, 0, j)),
                    pl.BlockSpec((nb, 1, tn), lambda i, j: (0, 0, j))] + [col(w) for w in w_outs],
        out_specs=pl.BlockSpec((tm, tn), lambda i, j: (i, j)),
        out_shape=jax.ShapeDtypeStruct((t, d), BF16),
        compiler_params=_params("parallel", "parallel"),
        name="merge",
    )(p, *ys, gate_up, gate_bias.reshape(nb, 1, d), *w_outs)


def _rope_tables(ctx_len, seq_len, dh):
    pairs = dh // 4
    rows = seq_len // GRID_W
    row = jnp.repeat(jnp.arange(rows), GRID_W).astype(F32)
    col = jnp.tile(jnp.arange(GRID_W), rows).astype(F32)
    inv = ROPE_THETA ** (-jnp.arange(pairs, dtype=F32) / pairs)
    ang = jnp.concatenate([row[:, None] * inv, col[:, None] * inv], axis=-1)
    cos, sin = jnp.cos(ang), jnp.sin(ang)
    cos = jnp.concatenate([jnp.ones((ctx_len, dh), F32), jnp.concatenate([cos, cos], axis=-1)], axis=0)
    sin = jnp.concatenate([jnp.zeros((ctx_len, dh), F32), jnp.concatenate([-sin, sin], axis=-1)], axis=0)
    return cos, sin


def kernel(x, c, ctx, c_ctx, ada_down, ada_up, ada_bias, norm_gains, w_in, conv_w, s5_lambda_re, s5_lambda_im, s5_log_step, s5_b_re, s5_b_im, s5_c_re, s5_c_im, s5_d, s5_w_glu, pool_w, pool_scale, qk_norm, gate_up, gate_bias, w_out_conv, w_out_s5, w_out_pool, w_out_attn, w_o, ffn_w_gate, ffn_w_up, ffn_w_down):
    batch, seq_len, d = x.shape
    ctx_len = ctx.shape[1]
    depth = w_in.shape[0]
    assert batch == 1 and c.shape[0] == 1
    t = ctx_len + seq_len
    width = conv_w.shape[-1]
    assert s5_d.shape[-1] == width and pool_scale.shape[-1] == width
    dh = qk_norm.shape[-1]
    q_width = w_out_attn.shape[1]
    rank = gate_up.shape[2]
    in_width = w_in.shape[-1]
    kv_width = (in_width - 5 * width - q_width - rank) // 2
    heads, kv_heads = q_width // dh, kv_width // dh
    groups, s_dim = s5_b_re.shape[2], s5_b_re.shape[-1]
    hidden = ffn_w_gate.shape[-1]
    q_col, k_col, v_col = 5 * width, 5 * width + q_width, 5 * width + q_width + kv_width
    gate_col = (v_col + kv_width) // rank
    assert q_col % dh == 0 and (v_col + kv_width) % rank == 0 and width % dh == 0

    tr = _tile(ctx_len, 256, 8)
    tm = _tile(t, 1280, 256)
    tm_half = _tile(t, 640, 128)
    tk_att = _tile(t, 1280, 256)
    tq = _tile(ctx_len, 256, 128)
    tn_ffn = _tile(hidden, 256, 128)
    ctx_tiles = ctx_len // tr

    half_perm = np.concatenate([np.arange(0, dh, 2), np.arange(1, dh, 2)])
    col_perm = np.arange(in_width)
    for h in range(heads + kv_heads):
        col_perm[q_col + h * dh:q_col + (h + 1) * dh] = q_col + h * dh + half_perm
    w_in_b = w_in[:, :, col_perm].astype(BF16)
    qk_gain = qk_norm[:, :, half_perm]
    nb = hidden // tn_ffn
    w_gu = jnp.stack([ffn_w_gate.reshape(depth, d, nb, tn_ffn), ffn_w_up.reshape(depth, d, nb, tn_ffn)],
                     axis=3).reshape(depth, d, 2 * hidden).astype(BF16)
    w_down_b = ffn_w_down.astype(BF16)
    w_o_b = w_o.astype(BF16)
    w_outs_b = [w.astype(BF16) for w in (w_out_conv, w_out_s5, w_out_pool, w_out_attn)]
    gate_up_b = gate_up.astype(BF16)
    pool_w_b = pool_w.astype(BF16)
    w_glu_b = s5_w_glu.astype(BF16)
    w_state, w_s5out, decay = s5_chunk_weights(s5_lambda_re, s5_lambda_im, s5_log_step,
                                               s5_b_re, s5_b_im, s5_c_re, s5_c_im)
    cos, sin = _rope_tables(ctx_len, seq_len, dh)

    cond = jnp.concatenate([c_ctx[None, :], c, jnp.zeros((6, d), F32)], axis=0)
    mods = ada_modulation(cond, ada_down, ada_up, ada_bias)
    mods = mods[:, :2].reshape(depth, 2, N_MOD, d)
    mods = jnp.concatenate([mods, jnp.zeros((depth, 2, 8 - N_MOD, d), F32)], axis=2)

    stream = jnp.concatenate([ctx[0], x[0]], axis=0)
    row_kw = dict(ctx_tiles=ctx_tiles, tr=tr)
    h = modulate(stream, mods[0], norm_gains[0], shift_row=0, scale_row=1, gain_row=0, **row_kw)
    n_chunks = t // S5_CHUNK
    for l in range(depth):
        p = matmul(h, w_in_b[l], tm=tm, tn=_tile(in_width, 512, 256))

        q_t = qk_prep(p, qk_gain[l, 0:1], cos, sin, col0=q_col // dh, heads=heads, tr=tk_att,
                      scale=dh ** -0.5, transpose=True)
        k_n = qk_prep(p, qk_gain[l, 1:2], cos, sin, col0=k_col // dh, heads=kv_heads, tr=tk_att,
                      scale=1.0, transpose=False)
        v_t = v_prep(p, col0=v_col // dh, heads=kv_heads, dh=dh, tk=tk_att)
        y_attn = flash_attention(q_t, k_n, v_t, tq=tq, ctx_len=ctx_len)

        y_conv, y_pool = conv_pool(p, conv_w[l], pool_w_b[l], pool_scale[l], width=width, tr=tr,
                                   ctx_len=ctx_len, seq_len=seq_len)

        u = p[:, 3 * width:4 * width].reshape(n_chunks, S5_CHUNK, groups, s_dim)
        u = jnp.transpose(u, (2, 0, 1, 3)).reshape(groups, n_chunks, S5_CHUNK * s_dim).astype(BF16)
        y_s5 = s5_mix(u, w_state[l], w_s5out[l], decay[l], ctx_chunks=ctx_len // S5_CHUNK,
                      gb=_tile(groups, 4, 1))
        y_s5 = jnp.transpose(y_s5.reshape(groups, n_chunks, S5_CHUNK, s_dim), (1, 2, 0, 3)).reshape(t, width)
        y_s5 = s5_glu(y_s5, p, s5_d[l], w_glu_b[l], width=width, tr=tm_half)

        merged = merge(p, [y_conv, y_s5, y_pool, y_attn], gate_up_b[l], gate_bias[l],
                       [w[l] for w in w_outs_b], gate_col=gate_col, tm=tm_half, tn=_tile(d, 512, 256))
        mix = matmul(merged, w_o_b[l], tm=tm, tn=_tile(d, 512, 256))
        stream, h = residual(stream, mix, mods[l], norm_gains[l], gate_row=2, post_row=1,
                             mod_next=mods[l], gains_next=norm_gains[l], nxt=(3, 4, 2), **row_kw)

        hid = ffn_up(h, w_gu[l], tm=tm, tn=tn_ffn)
        ffn = matmul(hid, w_down_b[l], tm=tm, tn=_tile(d, 512, 256), nk=2)
        if l + 1 < depth:
            stream, h = residual(stream, ffn, mods[l], norm_gains[l], gate_row=5, post_row=3,
                                 mod_next=mods[l + 1], gains_next=norm_gains[l + 1], nxt=(0, 1, 0), **row_kw)
        else:
            stream, _ = residual(stream, ffn, mods[l], norm_gains[l], gate_row=5, post_row=3, **row_kw)
    return stream[ctx_len:][None]
```

```python
import functools
import math

import jax
import jax.numpy as jnp
import numpy as np
from jax import lax
from jax.experimental import pallas as pl
from jax.experimental.pallas import tpu as pltpu

EPS = 1e-6
GRID_W = 64
ROPE_THETA = 10000.0
POOL_WINDOWS = (2, 4, 8, 16)
N_MOD = 6
S5_CHUNK = 16
HALO = 8
ONES_ROWS = 16
NEG_BIG = -1e30
V7X_VMEM_LIMIT = 56 * 1024 * 1024
F32 = jnp.float32
BF16 = jnp.bfloat16
HIGHEST = lax.Precision.HIGHEST


def _params(*sem):
    return pltpu.CompilerParams(dimension_semantics=sem, vmem_limit_bytes=V7X_VMEM_LIMIT)


def _tile(n, target, mult):
    best = None
    for t in range(mult, min(n, target) + 1, mult):
        if n % t == 0:
            best = t
    assert best is not None, (n, target, mult)
    return best


def _rms(x, gain):
    return x * lax.rsqrt(jnp.mean(x * x, axis=-1, keepdims=True) + EPS) * gain


def _sigmoid(x):
    return 1.0 / (1.0 + jnp.exp(-x))


def _ada_body(cond_ref, down_ref, up_ref, bias_ref, o_ref):
    s = cond_ref[...]
    s = s * _sigmoid(s)
    low = jnp.dot(s, down_ref[...], precision=HIGHEST, preferred_element_type=F32)
    o_ref[...] = jnp.dot(low, up_ref[...], precision=HIGHEST, preferred_element_type=F32) + bias_ref[...]


def ada_modulation(cond, down, up, bias):
    depth, d, rank = down.shape
    width = up.shape[-1]
    tn = _tile(width, 4096, 128)
    return pl.pallas_call(
        _ada_body,
        grid=(depth, width // tn),
        in_specs=[
            pl.BlockSpec((8, d), lambda l, j: (0, 0)),
            pl.BlockSpec((None, d, rank), lambda l, j: (l, 0, 0)),
            pl.BlockSpec((None, rank, tn), lambda l, j: (l, 0, j)),
            pl.BlockSpec((None, 1, tn), lambda l, j: (l, 0, j)),
        ],
        out_specs=pl.BlockSpec((None, 8, tn), lambda l, j: (l, 0, j)),
        out_shape=jax.ShapeDtypeStruct((depth, 8, width), F32),
        compiler_params=_params("arbitrary", "arbitrary"),
        name="ada_modulation",
    )(cond, down, up, bias.reshape(depth, 1, width))


def _modulated(x, mod, gains, rows):
    shift_row, scale_row, gain_row = rows
    h = _rms(x, gains[gain_row:gain_row + 1])
    return h * (1.0 + mod[scale_row:scale_row + 1]) + mod[shift_row:shift_row + 1]


def _modulate_body(x_ref, mod_ref, gain_ref, h_ref, *, rows):
    h_ref[...] = _modulated(x_ref[...], mod_ref[...], gain_ref[...], rows).astype(h_ref.dtype)


def modulate(x, mods, gains, layer, *, ctx_tiles, tr, rows):
    t, d = x.shape
    return pl.pallas_call(
        functools.partial(_modulate_body, rows=rows),
        grid=(t // tr,),
        in_specs=[
            pl.BlockSpec((tr, d), lambda i: (i, 0)),
            pl.BlockSpec((None, None, 8, d), lambda i: (layer, jnp.where(i < ctx_tiles, 0, 1), 0, 0)),
            pl.BlockSpec((None,) + gains.shape[1:], lambda i: (layer, 0, 0)),
        ],
        out_specs=pl.BlockSpec((tr, d), lambda i: (i, 0)),
        out_shape=jax.ShapeDtypeStruct((t, d), BF16),
        compiler_params=_params("parallel"),
        name="modulate",
    )(x, mods, gains)


def _residual_body(x_ref, y_ref, mod_ref, gain_ref, modn_ref, gainn_ref, xo_ref, *h_ref,
                   gate_row, post_row, nxt):
    mod = mod_ref[...]
    xn = x_ref[...] + mod[gate_row:gate_row + 1] * _rms(y_ref[...], gain_ref[post_row:post_row + 1])
    xo_ref[...] = xn
    if nxt is not None:
        h_ref[0][...] = _modulated(xn, modn_ref[...], gainn_ref[...], nxt).astype(h_ref[0].dtype)


def residual(x, y, mods, gains, layer, *, ctx_tiles, tr, gate_row, post_row, nxt=None, next_layer=None,
             first_tile=0):
    t, d = x.shape
    n_tiles = t // tr - first_tile
    if next_layer is None:
        next_layer = layer
    who = lambda i: jnp.where(i + first_tile < ctx_tiles, 0, 1)
    row_in = pl.BlockSpec((tr, d), lambda i: (i + first_tile, 0))
    row_out = pl.BlockSpec((tr, d), lambda i: (i, 0))
    mod_spec = lambda l: pl.BlockSpec((None, None, 8, d), lambda i: (l, who(i), 0, 0))
    gain_spec = lambda l: pl.BlockSpec((None,) + gains.shape[1:], lambda i: (l, 0, 0))
    out_specs = [row_out]
    out_shape = [jax.ShapeDtypeStruct((n_tiles * tr, d), F32)]
    if nxt is not None:
        out_specs.append(row_out)
        out_shape.append(jax.ShapeDtypeStruct((n_tiles * tr, d), BF16))
    out = pl.pallas_call(
        functools.partial(_residual_body, gate_row=gate_row, post_row=post_row, nxt=nxt),
        grid=(n_tiles,),
        in_specs=[row_in, row_in, mod_spec(layer), gain_spec(layer), mod_spec(next_layer),
                  gain_spec(next_layer)],
        out_specs=out_specs,
        out_shape=out_shape,
        compiler_params=_params("parallel"),
        name="residual_norm",
    )(x, y, mods, gains, mods, gains)
    return out if nxt is not None else (out[0], None)


def _matmul_body(a_ref, b_ref, o_ref, *, nk):
    r = jnp.dot(a_ref[...], b_ref[...], preferred_element_type=F32)
    if nk == 1:
        o_ref[...] = r.astype(o_ref.dtype)
    else:
        k = pl.program_id(2)

        @pl.when(k == 0)
        def _():
            o_ref[...] = r

        @pl.when(k > 0)
        def _():
            o_ref[...] += r


def matmul(a, b, layer, *, tm, tn, nk=1, out_dtype=F32):
    m, k = a.shape
    n = b.shape[-1]
    assert k % nk == 0 and (nk == 1 or out_dtype == F32)
    tk = k // nk
    return pl.pallas_call(
        functools.partial(_matmul_body, nk=nk),
        grid=(m // tm, n // tn, nk),
        in_specs=[
            pl.BlockSpec((tm, tk), lambda i, j, kk: (i, kk)),
            pl.BlockSpec((None, tk, tn), lambda i, j, kk: (layer, kk, j)),
        ],
        out_specs=pl.BlockSpec((tm, tn), lambda i, j, kk: (i, j)),
        out_shape=jax.ShapeDtypeStruct((m, n), out_dtype),
        compiler_params=_params("parallel", "parallel", "arbitrary"),
        name="matmul",
    )(a, b)


def _ffn_up_body(a_ref, wg_ref, wu_ref, o_ref):
    a = a_ref[...]
    g = jnp.dot(a, wg_ref[...], preferred_element_type=F32)
    u = jnp.dot(a, wu_ref[...], preferred_element_type=F32)
    o_ref[...] = (g * _sigmoid(g) * u).astype(o_ref.dtype)


def ffn_up(a, w_gate, w_up, layer, *, tm, tn):
    m, k = a.shape
    n = w_gate.shape[-1]
    w_spec = pl.BlockSpec((None, k, tn), lambda i, j: (layer, 0, j))
    return pl.pallas_call(
        _ffn_up_body,
        grid=(m // tm, n // tn),
        in_specs=[pl.BlockSpec((tm, k), lambda i, j: (i, 0)), w_spec, w_spec],
        out_specs=pl.BlockSpec((tm, tn), lambda i, j: (i, j)),
        out_shape=jax.ShapeDtypeStruct((m, n), BF16),
        compiler_params=_params("parallel", "parallel"),
        name="ffn_up",
    )(a, w_gate, w_up)


def _qk_prep_body(x_ref, gain_ref, cos_ref, sin_ref, o_ref, *, scale, transpose):
    y = _rms(x_ref[...], gain_ref[...])
    lanes = y.shape[-1]
    even = lax.broadcasted_iota(jnp.int32, (1, lanes), 1) % 2 == 0
    partner = jnp.where(even, pltpu.roll(y, lanes - 1, 1), pltpu.roll(y, 1, 1))
    y = (y * cos_ref[...] + partner * sin_ref[...]) * scale
    if transpose:
        o_ref[...] = y.T.astype(o_ref.dtype)
    else:
        o_ref[...] = y.astype(o_ref.dtype)


def qk_prep(p, gains, cos, sin, layer, which, *, col0, heads, tr, scale, transpose):
    t = p.shape[0]
    dh = gains.shape[-1]
    if transpose:
        out_spec = pl.BlockSpec((None, dh, tr), lambda i, h: (h, 0, i))
        out_shape = jax.ShapeDtypeStruct((heads, dh, t), BF16)
    else:
        out_spec = pl.BlockSpec((None, tr, dh), lambda i, h: (h, i, 0))
        out_shape = jax.ShapeDtypeStruct((heads, t, dh), BF16)
    return pl.pallas_call(
        functools.partial(_qk_prep_body, scale=scale, transpose=transpose),
        grid=(t // tr, heads),
        in_specs=[
            pl.BlockSpec((tr, dh), lambda i, h: (i, col0 + h)),
            pl.BlockSpec((None, None, 1, dh), lambda i, h: (layer, which, 0, 0)),
            pl.BlockSpec((tr, dh), lambda i, h: (i, 0)),
            pl.BlockSpec((tr, dh), lambda i, h: (i, 0)),
        ],
        out_specs=out_spec,
        out_shape=out_shape,
        compiler_params=_params("parallel", "parallel"),
        name="qk_prep",
    )(p, gains, cos, sin)


def _v_prep_body(x_ref, o_ref):
    dh = x_ref.shape[1]
    o_ref[:dh] = x_ref[...].T.astype(o_ref.dtype)
    first = lax.broadcasted_iota(jnp.int32, (ONES_ROWS, o_ref.shape[1]), 0) == 0
    o_ref[dh:] = jnp.where(first, 1.0, 0.0).astype(o_ref.dtype)


def v_prep(p, *, col0, heads, dh, tk):
    t = p.shape[0]
    return pl.pallas_call(
        _v_prep_body,
        grid=(t // tk, heads),
        in_specs=[pl.BlockSpec((tk, dh), lambda i, h: (i, col0 + h))],
        out_specs=pl.BlockSpec((None, None, dh + ONES_ROWS, tk), lambda i, h: (h, i, 0, 0)),
        out_shape=jax.ShapeDtypeStruct((heads, t // tk, dh + ONES_ROWS, tk), BF16),
        compiler_params=_params("parallel", "parallel"),
        name="v_prep",
    )(p)


def _flash_body(q_ref, k_ref, v_ref, o_ref, s_a, s_b, m_a, m_b,
                *, tq, tk, group, ctx_len, ctx_q_tiles, n_chunks):
    dh = q_ref.shape[1]
    rows = v_ref.shape[1]

    def scores(c, g, s_ref, m_ref, masked):
        k_c = k_ref[pl.ds(pl.multiple_of(c * tk, tk), tk), :]
        s = jnp.dot(k_c, q_ref[g], preferred_element_type=F32)
        if masked:
            key = c * tk + lax.broadcasted_iota(jnp.int32, (tk, 1), 0)
            s = jnp.where(key < ctx_len, s, NEG_BIG)
        s_ref[g] = s
        m_ref[g] = jnp.max(s, axis=0, keepdims=True)

    def absorb(c, g, s_ref, m_ref, carry):
        m, acc = carry
        m_new = jnp.maximum(m, m_ref[g])
        p = jnp.exp2(s_ref[g] - m_new).astype(BF16)
        acc = jnp.exp2(m - m_new) * acc + jnp.dot(v_ref[c], p, preferred_element_type=F32)
        return m_new, acc

    def run(chunks, masked):
        carry = tuple((jnp.full((1, tq), NEG_BIG, F32), jnp.zeros((rows, tq), F32)) for _ in range(group))
        for g in range(group):
            scores(0, g, s_a, m_a, masked)

        def half(c_next, c_cur, nxt, cur, carry):
            out = []
            for g in range(group):
                scores(c_next, g, *nxt, masked)
                out.append(absorb(c_cur, g, *cur, carry[g]))
            return tuple(out)

        def pair(c, carry):
            carry = half(c + 1, c, (s_b, m_b), (s_a, m_a), carry)
            return half(jnp.minimum(c + 2, chunks - 1), c + 1, (s_a, m_a), (s_b, m_b), carry)

        def quad(i, carry):
            return pair(4 * i + 2, pair(4 * i, carry))

        carry = lax.fori_loop(0, chunks // 4, quad, carry)
        if chunks % 4 >= 2:
            carry = pair(chunks // 4 * 4, carry)
        for g in range(group):
            m, acc = carry[g]
            if chunks % 2 == 1:
                m, acc = absorb(chunks - 1, g, s_a, m_a, (m, acc))
            o = acc[:dh] / acc[dh:dh + 1]
            o_ref[:, g * dh:(g + 1) * dh] = o.T.astype(o_ref.dtype)

    i = pl.program_id(1)

    @pl.when(i < ctx_q_tiles)
    def _():
        run(-(-ctx_len // tk), True)

    @pl.when(i >= ctx_q_tiles)
    def _():
        run(n_chunks, False)


def flash_attention(q_t, k, v_t, *, tq, ctx_len):
    heads, dh, t = q_t.shape
    kv, n_chunks, rows, tk = v_t.shape
    group = heads // kv
    assert ctx_len % tq == 0
    return pl.pallas_call(
        functools.partial(_flash_body, tq=tq, tk=tk, group=group, ctx_len=ctx_len,
                          ctx_q_tiles=ctx_len // tq, n_chunks=n_chunks),
        grid=(kv, t // tq),
        in_specs=[
            pl.BlockSpec((group, dh, tq), lambda h, i: (h, 0, i)),
            pl.BlockSpec((None, t, dh), lambda h, i: (h, 0, 0)),
            pl.BlockSpec((None, n_chunks, rows, tk), lambda h, i: (h, 0, 0, 0)),
        ],
        out_specs=pl.BlockSpec((tq, group * dh), lambda h, i: (i, h)),
        out_shape=jax.ShapeDtypeStruct((t, heads * dh), BF16),
        scratch_shapes=[pltpu.VMEM((group, tk, tq), F32), pltpu.VMEM((group, tk, tq), F32),
                        pltpu.VMEM((group, 1, tq), F32), pltpu.VMEM((group, 1, tq), F32)],
        compiler_params=_params("parallel", "parallel"),
        name="flash_attention",
    )(q_t, k, v_t)


def _convpool_body(ah_ref, ab_ref, ac_ref, pu_ref, ah_up, ac_up, pu_up, ah_dn, ac_dn, pu_dn,
                   cw_ref, pw_ref, ps_ref, yc_ref, yp_ref, xbuf, pbuf,
                   *, tr, ctx_tiles, n_tiles, ctx_len, seq_len):
    i = pl.program_id(0)
    has_up = jnp.logical_and(i != 0, i != ctx_tiles)
    has_dn = jnp.logical_and(i != ctx_tiles - 1, i != n_tiles - 1)

    xbuf[0:HALO] = jnp.where(has_up, ac_up[...] * ah_up[...], 0.0)
    xbuf[HALO:HALO + tr] = ac_ref[...] * ah_ref[...]
    xbuf[HALO + tr:HALO + tr + HALO] = jnp.where(has_dn, ac_dn[...] * ah_dn[...], 0.0)
    conv = (cw_ref[0:1] * xbuf[HALO - 1:HALO - 1 + tr] + cw_ref[1:2] * xbuf[HALO:HALO + tr]
            + cw_ref[2:3] * xbuf[HALO + 1:HALO + 1 + tr])
    yc_ref[...] = (ab_ref[...] * conv).astype(yc_ref.dtype)

    pbuf[0:HALO] = jnp.where(has_up, pu_up[...], 0.0)
    pbuf[HALO:HALO + tr] = pu_ref[...]
    pbuf[HALO + tr:HALO + tr + HALO] = jnp.where(has_dn, pu_dn[...], 0.0)
    in_ctx = i < ctx_tiles
    pos = lax.broadcasted_iota(jnp.int32, (tr, 1), 0) + jnp.where(in_ctx, i, i - ctx_tiles) * tr
    length = jnp.where(in_ctx, ctx_len, seq_len)
    pg = pw_ref.shape[-1]
    for g, win in enumerate(POOL_WINDOWS):
        cols = slice(g * pg, (g + 1) * pg)
        total = None
        for s in range(-(win // 2), win - win // 2):
            piece = pbuf[HALO + s:HALO + s + tr, cols]
            total = piece if total is None else total + piece
        count = jnp.minimum(pos - win // 2 + win, length) - jnp.maximum(pos - win // 2, 0)
        pooled = total / count.astype(F32) - pu_ref[:, cols]
        y = jnp.dot(pooled.astype(BF16), pw_ref[g], preferred_element_type=F32) * ps_ref[:, cols]
        yp_ref[:, cols] = y.astype(yp_ref.dtype)


def conv_pool(p, conv_w, pool_w, pool_scale, layer, *, width, tr, ctx_len, seq_len):
    t = p.shape[0]
    n_tiles = t // tr
    ctx_tiles = ctx_len // tr
    assert ctx_len % tr == 0 and tr % HALO == 0 and max(POOL_WINDOWS) // 2 <= HALO
    hb = tr // HALO
    main = lambda c: pl.BlockSpec((tr, width), lambda i: (i, c))
    up = lambda c: pl.BlockSpec((HALO, width), lambda i: (jnp.maximum(i * hb - 1, 0), c))
    dn = lambda c: pl.BlockSpec((HALO, width), lambda i: (jnp.minimum((i + 1) * hb, t // HALO - 1), c))
    return pl.pallas_call(
        functools.partial(_convpool_body, tr=tr, ctx_tiles=ctx_tiles, n_tiles=n_tiles,
                          ctx_len=ctx_len, seq_len=seq_len),
        grid=(n_tiles,),
        in_specs=[main(0), main(1), main(2), main(4), up(0), up(2), up(4), dn(0), dn(2), dn(4),
                  pl.BlockSpec((None,) + conv_w.shape[1:], lambda i: (layer, 0, 0)),
                  pl.BlockSpec((None,) + pool_w.shape[1:], lambda i: (layer, 0, 0, 0)),
                  pl.BlockSpec((None, 1, width), lambda i: (layer, 0, 0))],
        out_specs=[pl.BlockSpec((tr, width), lambda i: (i, 0))] * 2,
        out_shape=[jax.ShapeDtypeStruct((t, width), BF16)] * 2,
        scratch_shapes=[pltpu.VMEM((tr + 2 * HALO, width), F32)] * 2,
        compiler_params=_params("parallel"),
        name="conv_pool",
    )(p, p, p, p, p, p, p, p, p, p, conv_w, pool_w, pool_scale.reshape(-1, 1, width))


def s5_chunk_weights(lam_re, lam_im, log_step, b_re, b_im, c_re, c_im):
    tc = S5_CHUNK
    lam = lax.complex(lam_re.astype(F32), lam_im.astype(F32))
    z = lam * jnp.exp(log_step.astype(F32))[..., None]
    lam_bar = jnp.exp(z)
    b_bar = ((lam_bar - 1.0) / lam)[..., None] * lax.complex(b_re.astype(F32), b_im.astype(F32))
    c = lax.complex(c_re.astype(F32), c_im.astype(F32))
    depth, _, groups, p_dim, s_dim = b_bar.shape
    powers = jnp.exp(z[..., None, :] * jnp.arange(tc + 1, dtype=F32)[:, None])

    cp = c[..., None, :, :] * powers[..., :tc, None, :]
    kern = (jnp.einsum('ldgtop,ldgpi->ldgtoi', jnp.real(cp), jnp.real(b_bar), precision=HIGHEST)
            - jnp.einsum('ldgtop,ldgpi->ldgtoi', jnp.imag(cp), jnp.imag(b_bar), precision=HIGHEST))
    step = np.arange(tc)
    lag = step[None, :] - step[:, None]
    k_f = kern[:, 0][:, :, np.clip(lag, 0, tc - 1)] * (lag >= 0)[:, :, None, None]
    k_b = kern[:, 1][:, :, np.clip(-lag, 0, tc - 1)] * (lag <= 0)[:, :, None, None]
    toeplitz = jnp.transpose(k_f + k_b, (0, 1, 2, 5, 3, 4)).reshape(depth, groups, tc * s_dim, tc * s_dim)

    f_f = c[:, 0][:, :, None] * powers[:, 0, :, 1:tc + 1, None, :]
    f_b = c[:, 1][:, :, None] * powers[:, 1, :, tc:0:-1, None, :]
    rows = lambda a: jnp.transpose(a, (0, 1, 4, 2, 3)).reshape(depth, groups, p_dim, tc * s_dim)
    w_out = jnp.concatenate([toeplitz, rows(jnp.real(f_f)), rows(jnp.real(f_b)),
                             rows(-jnp.imag(f_f)), rows(-jnp.imag(f_b))], axis=2)

    e_f = powers[:, 0, :, tc - 1::-1][:, :, :tc, :, None] * b_bar[:, 0][:, :, None]
    e_b = powers[:, 1, :, :tc, :, None] * b_bar[:, 1][:, :, None]
    cols = lambda a: jnp.transpose(a, (0, 1, 2, 4, 3)).reshape(depth, groups, tc * s_dim, p_dim)
    w_state = jnp.concatenate([cols(jnp.real(e_f)), cols(jnp.real(e_b)),
                               cols(jnp.imag(e_f)), cols(jnp.imag(e_b))], axis=3)

    last = powers[..., tc, :]
    decay = jnp.stack([jnp.concatenate([jnp.real(last[:, 0]), jnp.real(last[:, 1])], axis=-1),
                       jnp.concatenate([jnp.imag(last[:, 0]), jnp.imag(last[:, 1])], axis=-1)], axis=2)
    return w_state.astype(BF16), w_out.astype(BF16), decay


def _s5_body(u_ref, ws_ref, wo_ref, decay_ref, y_ref, s_scr, hf_scr, hb_scr, *, gb, n, ctx_chunks):
    half = decay_ref.shape[-1]
    cs = u_ref.shape[-1]
    is_fwd = lax.broadcasted_iota(jnp.int32, (1, half), 1) < half // 2
    for g in range(gb):
        s_scr[g] = jnp.dot(u_ref[g], ws_ref[g], preferred_element_type=F32)

    def scan(j, carry):
        jb = jnp.where(j < ctx_chunks, ctx_chunks - 1 - j, n - 1 + ctx_chunks - j)
        out = []
        for g in range(gb):
            c_re, c_im = carry[g]
            hf_scr[g, pl.ds(j, 1), :] = jnp.concatenate([c_re, c_im], axis=1)
            hb_scr[g, pl.ds(jb, 1), :] = jnp.concatenate([c_re, c_im], axis=1)
            row_f = s_scr[g, pl.ds(j, 1), :]
            row_b = s_scr[g, pl.ds(jb, 1), :]
            s_re = jnp.where(is_fwd, row_f[:, :half], row_b[:, :half])
            s_im = jnp.where(is_fwd, row_f[:, half:], row_b[:, half:])
            a_re, a_im = decay_ref[g, 0:1, :], decay_ref[g, 1:2, :]
            out.append((a_re * c_re - a_im * c_im + s_re, a_re * c_im + a_im * c_re + s_im))
        return tuple(out)

    zero = jnp.zeros((1, half), F32)
    lax.fori_loop(0, n, scan, tuple((zero, zero) for _ in range(gb)))

    for g in range(gb):
        h_re = jnp.where(is_fwd, hf_scr[g, :, :half], hb_scr[g, :, :half]).astype(BF16)
        h_im = jnp.where(is_fwd, hf_scr[g, :, half:], hb_scr[g, :, half:]).astype(BF16)
        y_ref[g] = (jnp.dot(u_ref[g], wo_ref[g, :cs], preferred_element_type=F32)
                    + jnp.dot(h_re, wo_ref[g, cs:cs + half], preferred_element_type=F32)
                    + jnp.dot(h_im, wo_ref[g, cs + half:], preferred_element_type=F32))


def s5_mix(u_chunks, w_state, w_out, decay, layer, *, ctx_chunks, gb):
    groups, n, cs = u_chunks.shape
    st = w_state.shape[-1]
    return pl.pallas_call(
        functools.partial(_s5_body, gb=gb, n=n, ctx_chunks=ctx_chunks),
        grid=(groups // gb,),
        in_specs=[
            pl.BlockSpec((gb, n, cs), lambda i: (i, 0, 0)),
            pl.BlockSpec((None, gb, cs, st), lambda i: (layer, i, 0, 0)),
            pl.BlockSpec((None, gb, cs + st, cs), lambda i: (layer, i, 0, 0)),
            pl.BlockSpec((None, gb, 2, st // 2), lambda i: (layer, i, 0, 0)),
        ],
        out_specs=pl.BlockSpec((gb, n, cs), lambda i: (i, 0, 0)),
        out_shape=jax.ShapeDtypeStruct((groups, n, cs), F32),
        scratch_shapes=[pltpu.VMEM((gb, n, st), F32)] * 3,
        compiler_params=_params("parallel"),
        name="s5_mix",
    )(u_chunks, w_state, w_out, decay)


def _s5_glu_body(y_ref, u_ref, d_ref, w_ref, o_ref):
    z = jax.nn.gelu(y_ref[...] + d_ref[...] * u_ref[...])
    gate = _sigmoid(jnp.dot(z.astype(BF16), w_ref[...], preferred_element_type=F32))
    o_ref[...] = (z * gate).astype(o_ref.dtype)


def s5_glu(y, p, d, w_glu, layer, *, width, tr):
    t = p.shape[0]
    return pl.pallas_call(
        _s5_glu_body,
        grid=(t // tr,),
        in_specs=[
            pl.BlockSpec((tr, width), lambda i: (i, 0)),
            pl.BlockSpec((tr, width), lambda i: (i, 3)),
            pl.BlockSpec((None, 1, width), lambda i: (layer, 0, 0)),
            pl.BlockSpec((None, width, width), lambda i: (layer, 0, 0)),
        ],
        out_specs=pl.BlockSpec((tr, width), lambda i: (i, 0)),
        out_shape=jax.ShapeDtypeStruct((t, width), BF16),
        compiler_params=_params("parallel"),
        name="s5_glu",
    )(y, p, d.reshape(-1, 1, width), w_glu)


def _merge_body(gl_ref, yc_ref, ys_ref, yp_ref, ya_ref, gu_ref, gb_ref, wc_ref, ws_ref, wp_ref, wa_ref,
                o_ref):
    g_low = gl_ref[...].astype(BF16)
    total = None
    for b, (y_ref, w_ref) in enumerate(((yc_ref, wc_ref), (ys_ref, ws_ref), (yp_ref, wp_ref),
                                        (ya_ref, wa_ref))):
        gate = _sigmoid(jnp.dot(g_low, gu_ref[b], preferred_element_type=F32) + gb_ref[b])
        term = gate * jnp.dot(y_ref[...], w_ref[...], preferred_element_type=F32)
        total = term if total is None else total + term
    o_ref[...] = total.astype(o_ref.dtype)


def merge(p, ys, gate_up, gate_bias, w_outs, layer, *, gate_col, tm, tn):
    t = p.shape[0]
    _, nb, rank, d = gate_up.shape
    row = lambda a: pl.BlockSpec((tm, a.shape[1]), lambda i, j: (i, 0))
    col = lambda a: pl.BlockSpec((None, a.shape[1], tn), lambda i, j: (layer, 0, j))
    return pl.pallas_call(
        _merge_body,
        grid=(t // tm, d // tn),
        in_specs=[pl.BlockSpec((tm, rank), lambda i, j: (i, gate_col))] + [row(y) for y in ys]
                 + [pl.BlockSpec((None, nb, rank, tn), lambda i, j: (layer, 0, 0, j)),
                    pl.BlockSpec((None, nb, 1, tn), lambda i, j: (layer, 0, 0, j))] + [col(w) for w in w_outs],
        out_specs=pl.BlockSpec((tm, tn), lambda i, j: (i, j)),
        out_shape=jax.ShapeDtypeStruct((t, d), BF16),
        compiler_params=_params("parallel", "parallel"),
        name="merge",
    )(p, *ys, gate_up, gate_bias.reshape(-1, nb, 1, d), *w_outs)


def _rope_tables(ctx_len, seq_len, dh):
    pairs = dh // 4
    rows = seq_len // GRID_W
    row = jnp.repeat(jnp.arange(rows), GRID_W).astype(F32)
    col = jnp.tile(jnp.arange(GRID_W), rows).astype(F32)
    inv = ROPE_THETA ** (-jnp.arange(pairs, dtype=F32) / pairs)
    ang = jnp.concatenate([row[:, None] * inv, col[:, None] * inv], axis=-1)
    cos = jnp.repeat(jnp.cos(ang), 2, axis=-1)
    sin = jnp.stack([-jnp.sin(ang), jnp.sin(ang)], axis=-1).reshape(seq_len, dh)
    cos = jnp.concatenate([jnp.ones((ctx_len, dh), F32), cos], axis=0)
    sin = jnp.concatenate([jnp.zeros((ctx_len, dh), F32), sin], axis=0)
    return cos, sin


def kernel(x, c, ctx, c_ctx, ada_down, ada_up, ada_bias, norm_gains, w_in, conv_w, s5_lambda_re, s5_lambda_im, s5_log_step, s5_b_re, s5_b_im, s5_c_re, s5_c_im, s5_d, s5_w_glu, pool_w, pool_scale, qk_norm, gate_up, gate_bias, w_out_conv, w_out_s5, w_out_pool, w_out_attn, w_o, ffn_w_gate, ffn_w_up, ffn_w_down):
    batch, seq_len, d = x.shape
    ctx_len = ctx.shape[1]
    depth = w_in.shape[0]
    assert batch == 1 and c.shape[0] == 1
    t = ctx_len + seq_len
    width = conv_w.shape[-1]
    assert s5_d.shape[-1] == width and pool_scale.shape[-1] == width
    dh = qk_norm.shape[-1]
    q_width = w_out_attn.shape[1]
    rank = gate_up.shape[2]
    in_width = w_in.shape[-1]
    kv_width = (in_width - 5 * width - q_width - rank) // 2
    heads, kv_heads = q_width // dh, kv_width // dh
    groups, s_dim = s5_b_re.shape[2], s5_b_re.shape[-1]
    hidden = ffn_w_gate.shape[-1]
    q_col, k_col, v_col = 5 * width, 5 * width + q_width, 5 * width + q_width + kv_width
    gate_col = (v_col + kv_width) // rank
    assert q_col % dh == 0 and (v_col + kv_width) % rank == 0 and width % dh == 0

    tr = _tile(ctx_len, 256, 8)
    tm = _tile(t, 1280, 256)
    tm_half = _tile(t, 640, 128)
    tk_att = _tile(t, 1280, 256)
    tq = _tile(ctx_len, 256, 128)
    ctx_tiles = ctx_len // tr

    w_in_b, w_o_b, w_gate_b, w_up_b, w_down_b = (w.astype(BF16) for w in (w_in, w_o, ffn_w_gate, ffn_w_up,
                                                                         ffn_w_down))
    w_outs_b = [w.astype(BF16) for w in (w_out_conv, w_out_s5, w_out_pool, w_out_attn)]
    gate_up_b, pool_w_b, w_glu_b = gate_up.astype(BF16), pool_w.astype(BF16), s5_w_glu.astype(BF16)
    w_state, w_s5out, decay = s5_chunk_weights(s5_lambda_re, s5_lambda_im, s5_log_step,
                                               s5_b_re, s5_b_im, s5_c_re, s5_c_im)
    cos, sin = _rope_tables(ctx_len, seq_len, dh)
    qk_gains = qk_norm.reshape(depth, 2, 1, dh)

    cond = jnp.concatenate([c_ctx[None, :], c, jnp.zeros((6, d), F32)], axis=0)
    mods = ada_modulation(cond, ada_down, ada_up, ada_bias)
    mods = mods[:, :2].reshape(depth, 2, N_MOD, d)
    mods = jnp.concatenate([mods, jnp.zeros((depth, 2, 8 - N_MOD, d), F32)], axis=2)

    stream = jnp.concatenate([ctx[0], x[0]], axis=0)
    row_kw = dict(ctx_tiles=ctx_tiles, tr=tr)
    h = modulate(stream, mods, norm_gains, 0, rows=(0, 1, 0), **row_kw)
    n_chunks = t // S5_CHUNK
    for l in range(depth):
        p = matmul(h, w_in_b, l, tm=tm, tn=_tile(in_width, 512, 256))

        q_t = qk_prep(p, qk_gains, cos, sin, l, 0, col0=q_col // dh, heads=heads, tr=tk_att,
                      scale=dh ** -0.5 * math.log2(math.e), transpose=True)
        k_n = qk_prep(p, qk_gains, cos, sin, l, 1, col0=k_col // dh, heads=kv_heads, tr=tk_att,
                      scale=1.0, transpose=False)
        v_t = v_prep(p, col0=v_col // dh, heads=kv_heads, dh=dh, tk=tk_att)
        y_attn = flash_attention(q_t, k_n, v_t, tq=tq, ctx_len=ctx_len)

        y_conv, y_pool = conv_pool(p, conv_w, pool_w_b, pool_scale, l, width=width, tr=tr,
                                   ctx_len=ctx_len, seq_len=seq_len)

        u = p[:, 3 * width:4 * width].reshape(n_chunks, S5_CHUNK, groups, s_dim)
        u = jnp.transpose(u, (2, 0, 1, 3)).reshape(groups, n_chunks, S5_CHUNK * s_dim).astype(BF16)
        y_s5 = s5_mix(u, w_state, w_s5out, decay, l, ctx_chunks=ctx_len // S5_CHUNK, gb=_tile(groups, 4, 1))
        y_s5 = jnp.transpose(y_s5.reshape(groups, n_chunks, S5_CHUNK, s_dim), (1, 2, 0, 3)).reshape(t, width)
        y_s5 = s5_glu(y_s5, p, s5_d, w_glu_b, l, width=width, tr=tm_half)

        merged = merge(p, [y_conv, y_s5, y_pool, y_attn], gate_up_b, gate_bias, w_outs_b, l,
                       gate_col=gate_col, tm=tm_half, tn=_tile(d, 512, 256))
        mix = matmul(merged, w_o_b, l, tm=tm, tn=_tile(d, 512, 256))
        stream, h = residual(stream, mix, mods, norm_gains, l, gate_row=2, post_row=1, nxt=(3, 4, 2), **row_kw)

        hid = ffn_up(h, w_gate_b, w_up_b, l, tm=tm, tn=_tile(hidden, 256, 128))
        ffn = matmul(hid, w_down_b, l, tm=tm, tn=_tile(d, 512, 256), nk=2)
        if l + 1 < depth:
            stream, h = residual(stream, ffn, mods, norm_gains, l, gate_row=5, post_row=3, nxt=(0, 1, 0),
                                 next_layer=l + 1, **row_kw)
        else:
            stream, _ = residual(stream, ffn, mods, norm_gains, l, gate_row=5, post_row=3,
                                 first_tile=ctx_tiles, **row_kw)
    return stream[None]
```

```python
import functools
import math

import jax
import jax.numpy as jnp
import numpy as np
from jax import lax
from jax.experimental import pallas as pl
from jax.experimental.pallas import tpu as pltpu

EPS = 1e-6
GRID_W = 64
ROPE_THETA = 10000.0
POOL_WINDOWS = (2, 4, 8, 16)
N_MOD = 6
S5_CHUNK = 16
HALO = 16
LANES = 128
ONES_ROWS = 16
NEG_BIG = -1e30
V7X_VMEM_LIMIT = 56 * 1024 * 1024
F32 = jnp.float32
BF16 = jnp.bfloat16
HIGHEST = lax.Precision.HIGHEST


def _params(*sem):
    return pltpu.CompilerParams(dimension_semantics=sem, vmem_limit_bytes=V7X_VMEM_LIMIT)


def _tile(n, target, mult):
    best = None
    for t in range(mult, min(n, target) + 1, mult):
        if n % t == 0:
            best = t
    assert best is not None, (n, target, mult)
    return best


def _rms(x, gain):
    return x * lax.rsqrt(jnp.mean(x * x, axis=-1, keepdims=True) + EPS) * gain


def _sigmoid(x):
    return 1.0 / (1.0 + jnp.exp(-x))


def _ada_body(cond_ref, down_ref, up_ref, bias_ref, o_ref):
    s = cond_ref[...]
    s = s * _sigmoid(s)
    low = jnp.dot(s, down_ref[...], precision=HIGHEST, preferred_element_type=F32)
    o_ref[...] = jnp.dot(low, up_ref[...], precision=HIGHEST, preferred_element_type=F32) + bias_ref[...]


def ada_modulation(cond, down, up, bias):
    depth, d, rank = down.shape
    width = up.shape[-1]
    tn = _tile(width, 4096, 128)
    return pl.pallas_call(
        _ada_body,
        grid=(depth, width // tn),
        in_specs=[
            pl.BlockSpec((8, d), lambda l, j: (0, 0)),
            pl.BlockSpec((None, d, rank), lambda l, j: (l, 0, 0)),
            pl.BlockSpec((None, rank, tn), lambda l, j: (l, 0, j)),
            pl.BlockSpec((None, 1, tn), lambda l, j: (l, 0, j)),
        ],
        out_specs=pl.BlockSpec((None, 8, tn), lambda l, j: (l, 0, j)),
        out_shape=jax.ShapeDtypeStruct((depth, 8, width), F32),
        compiler_params=_params("arbitrary", "arbitrary"),
        name="ada_modulation",
    )(cond, down, up, bias.reshape(depth, 1, width))


def _modulated(x, mod, gains, rows):
    shift_row, scale_row, gain_row = rows
    h = _rms(x, gains[gain_row:gain_row + 1])
    return h * (1.0 + mod[scale_row:scale_row + 1]) + mod[shift_row:shift_row + 1]


def _modulate_body(x_ref, mod_ref, gain_ref, h_ref, *, rows):
    h_ref[...] = _modulated(x_ref[...], mod_ref[...], gain_ref[...], rows).astype(h_ref.dtype)


def modulate(x, mods, gains, layer, *, ctx_tiles, tr, rows):
    t, d = x.shape
    return pl.pallas_call(
        functools.partial(_modulate_body, rows=rows),
        grid=(t // tr,),
        in_specs=[
            pl.BlockSpec((tr, d), lambda i: (i, 0)),
            pl.BlockSpec((None, None, 8, d), lambda i: (layer, jnp.where(i < ctx_tiles, 0, 1), 0, 0)),
            pl.BlockSpec((None,) + gains.shape[1:], lambda i: (layer, 0, 0)),
        ],
        out_specs=pl.BlockSpec((tr, d), lambda i: (i, 0)),
        out_shape=jax.ShapeDtypeStruct((t, d), BF16),
        compiler_params=_params("parallel"),
        name="modulate",
    )(x, mods, gains)


def _residual_body(x_ref, y_ref, mod_ref, gain_ref, modn_ref, gainn_ref, xo_ref, *h_ref,
                   gate_row, post_row, nxt):
    mod = mod_ref[...]
    xn = x_ref[...] + mod[gate_row:gate_row + 1] * _rms(y_ref[...].astype(F32), gain_ref[post_row:post_row + 1])
    xo_ref[...] = xn
    if nxt is not None:
        h_ref[0][...] = _modulated(xn, modn_ref[...], gainn_ref[...], nxt).astype(h_ref[0].dtype)


def residual(x, y, mods, gains, layer, *, ctx_tiles, tr, gate_row, post_row, nxt=None, next_layer=None,
             first_tile=0):
    t, d = x.shape
    n_tiles = t // tr - first_tile
    if next_layer is None:
        next_layer = layer
    who = lambda i: jnp.where(i + first_tile < ctx_tiles, 0, 1)
    row_in = pl.BlockSpec((tr, d), lambda i: (i + first_tile, 0))
    row_out = pl.BlockSpec((tr, d), lambda i: (i, 0))
    mod_spec = lambda l: pl.BlockSpec((None, None, 8, d), lambda i: (l, who(i), 0, 0))
    gain_spec = lambda l: pl.BlockSpec((None,) + gains.shape[1:], lambda i: (l, 0, 0))
    out_specs = [row_out]
    out_shape = [jax.ShapeDtypeStruct((n_tiles * tr, d), F32)]
    if nxt is not None:
        out_specs.append(row_out)
        out_shape.append(jax.ShapeDtypeStruct((n_tiles * tr, d), BF16))
    out = pl.pallas_call(
        functools.partial(_residual_body, gate_row=gate_row, post_row=post_row, nxt=nxt),
        grid=(n_tiles,),
        in_specs=[row_in, row_in, mod_spec(layer), gain_spec(layer), mod_spec(next_layer),
                  gain_spec(next_layer)],
        out_specs=out_specs,
        out_shape=out_shape,
        compiler_params=_params("parallel"),
        name="residual_norm",
    )(x, y, mods, gains, mods, gains)
    return out if nxt is not None else (out[0], None)


def _matmul_body(a_ref, b_ref, o_ref, *, nk):
    r = jnp.dot(a_ref[...], b_ref[...], preferred_element_type=F32)
    if nk == 1:
        o_ref[...] = r.astype(o_ref.dtype)
    else:
        k = pl.program_id(2)

        @pl.when(k == 0)
        def _():
            o_ref[...] = r

        @pl.when(k > 0)
        def _():
            o_ref[...] += r


def matmul(a, b, layer, *, tm, tn, nk=1, out_dtype=F32):
    m, k = a.shape
    n = b.shape[-1]
    assert k % nk == 0 and (nk == 1 or out_dtype == F32)
    tk = k // nk
    return pl.pallas_call(
        functools.partial(_matmul_body, nk=nk),
        grid=(m // tm, n // tn, nk),
        in_specs=[
            pl.BlockSpec((tm, tk), lambda i, j, kk: (i, kk)),
            pl.BlockSpec((None, tk, tn), lambda i, j, kk: (layer, kk, j)),
        ],
        out_specs=pl.BlockSpec((tm, tn), lambda i, j, kk: (i, j)),
        out_shape=jax.ShapeDtypeStruct((m, n), out_dtype),
        compiler_params=_params("parallel", "parallel", "arbitrary"),
        name="matmul",
    )(a, b)


def _ffn_up_body(a_ref, wg_ref, wu_ref, o_ref):
    a = a_ref[...]
    g = jnp.dot(a, wg_ref[...], preferred_element_type=F32)
    u = jnp.dot(a, wu_ref[...], preferred_element_type=F32)
    o_ref[...] = (g * _sigmoid(g) * u).astype(o_ref.dtype)


def ffn_up(a, w_gate, w_up, layer, *, tm, tn):
    m, k = a.shape
    n = w_gate.shape[-1]
    w_spec = pl.BlockSpec((None, k, tn), lambda i, j: (layer, 0, j))
    return pl.pallas_call(
        _ffn_up_body,
        grid=(m // tm, n // tn),
        in_specs=[pl.BlockSpec((tm, k), lambda i, j: (i, 0)), w_spec, w_spec],
        out_specs=pl.BlockSpec((tm, tn), lambda i, j: (i, j)),
        out_shape=jax.ShapeDtypeStruct((m, n), BF16),
        compiler_params=_params("parallel", "parallel"),
        name="ffn_up",
    )(a, w_gate, w_up)


def _qk_prep_body(x_ref, gain_ref, cos_ref, sin_ref, o_ref, *, scale, transpose):
    y = _rms(x_ref[...].astype(F32), gain_ref[...])
    lanes = y.shape[-1]
    even = lax.broadcasted_iota(jnp.int32, (1, lanes), 1) % 2 == 0
    partner = jnp.where(even, pltpu.roll(y, lanes - 1, 1), pltpu.roll(y, 1, 1))
    y = (y * cos_ref[...] + partner * sin_ref[...]) * scale
    if transpose:
        o_ref[...] = y.T.astype(o_ref.dtype)
    else:
        o_ref[...] = y.astype(o_ref.dtype)


def qk_prep(p, gains, cos, sin, layer, which, *, col0, heads, tr, scale, transpose):
    t = p.shape[0]
    dh = gains.shape[-1]
    if transpose:
        out_spec = pl.BlockSpec((None, dh, tr), lambda i, h: (h, 0, i))
        out_shape = jax.ShapeDtypeStruct((heads, dh, t), BF16)
    else:
        out_spec = pl.BlockSpec((None, tr, dh), lambda i, h: (h, i, 0))
        out_shape = jax.ShapeDtypeStruct((heads, t, dh), BF16)
    return pl.pallas_call(
        functools.partial(_qk_prep_body, scale=scale, transpose=transpose),
        grid=(t // tr, heads),
        in_specs=[
            pl.BlockSpec((tr, dh), lambda i, h: (i, col0 + h)),
            pl.BlockSpec((None, None, 1, dh), lambda i, h: (layer, which, 0, 0)),
            pl.BlockSpec((tr, dh), lambda i, h: (i, 0)),
            pl.BlockSpec((tr, dh), lambda i, h: (i, 0)),
        ],
        out_specs=out_spec,
        out_shape=out_shape,
        compiler_params=_params("parallel", "parallel"),
        name="qk_prep",
    )(p, gains, cos, sin)


def _v_prep_body(x_ref, o_ref):
    dh = x_ref.shape[1]
    o_ref[:dh] = x_ref[...].astype(F32).T.astype(o_ref.dtype)
    first = lax.broadcasted_iota(jnp.int32, (ONES_ROWS, o_ref.shape[1]), 0) == 0
    o_ref[dh:] = jnp.where(first, 1.0, 0.0).astype(o_ref.dtype)


def v_prep(p, *, col0, heads, dh, tk):
    t = p.shape[0]
    return pl.pallas_call(
        _v_prep_body,
        grid=(t // tk, heads),
        in_specs=[pl.BlockSpec((tk, dh), lambda i, h: (i, col0 + h))],
        out_specs=pl.BlockSpec((None, None, dh + ONES_ROWS, tk), lambda i, h: (h, i, 0, 0)),
        out_shape=jax.ShapeDtypeStruct((heads, t // tk, dh + ONES_ROWS, tk), BF16),
        compiler_params=_params("parallel", "parallel"),
        name="v_prep",
    )(p)


def _flash_body(q_ref, k_ref, v_ref, o_ref, s_a, s_b, m_a, m_b,
                *, tq, tk, group, ctx_len, ctx_q_tiles, n_chunks):
    dh = q_ref.shape[1]
    rows = v_ref.shape[1]

    def scores(c, g, s_ref, m_ref, masked):
        k_c = k_ref[pl.ds(pl.multiple_of(c * tk, tk), tk), :]
        s = jnp.dot(k_c, q_ref[g], preferred_element_type=F32)
        if masked:
            key = c * tk + lax.broadcasted_iota(jnp.int32, (tk, 1), 0)
            s = jnp.where(key < ctx_len, s, NEG_BIG)
        s_ref[g] = s
        m_ref[g] = jnp.max(s, axis=0, keepdims=True)

    def absorb(c, g, s_ref, m_ref, carry):
        m, acc = carry
        m_new = jnp.maximum(m, m_ref[g])
        p = jnp.exp2(s_ref[g] - m_new).astype(BF16)
        acc = jnp.exp2(m - m_new) * acc + jnp.dot(v_ref[c], p, preferred_element_type=F32)
        return m_new, acc

    def run(chunks, masked):
        carry = tuple((jnp.full((1, tq), NEG_BIG, F32), jnp.zeros((rows, tq), F32)) for _ in range(group))
        for g in range(group):
            scores(0, g, s_a, m_a, masked)

        def half(c_next, c_cur, nxt, cur, carry):
            out = []
            for g in range(group):
                scores(c_next, g, *nxt, masked)
                out.append(absorb(c_cur, g, *cur, carry[g]))
            return tuple(out)

        def pair(c, carry):
            carry = half(c + 1, c, (s_b, m_b), (s_a, m_a), carry)
            return half(jnp.minimum(c + 2, chunks - 1), c + 1, (s_a, m_a), (s_b, m_b), carry)

        def quad(i, carry):
            return pair(4 * i + 2, pair(4 * i, carry))

        carry = lax.fori_loop(0, chunks // 4, quad, carry)
        if chunks % 4 >= 2:
            carry = pair(chunks // 4 * 4, carry)
        for g in range(group):
            m, acc = carry[g]
            if chunks % 2 == 1:
                m, acc = absorb(chunks - 1, g, s_a, m_a, (m, acc))
            o = acc[:dh] / acc[dh:dh + 1]
            o_ref[:, g * dh:(g + 1) * dh] = o.T.astype(o_ref.dtype)

    i = pl.program_id(1)

    @pl.when(i < ctx_q_tiles)
    def _():
        run(-(-ctx_len // tk), True)

    @pl.when(i >= ctx_q_tiles)
    def _():
        run(n_chunks, False)


def flash_attention(q_t, k, v_t, *, tq, ctx_len):
    heads, dh, t = q_t.shape
    kv, n_chunks, rows, tk = v_t.shape
    group = heads // kv
    assert ctx_len % tq == 0
    return pl.pallas_call(
        functools.partial(_flash_body, tq=tq, tk=tk, group=group, ctx_len=ctx_len,
                          ctx_q_tiles=ctx_len // tq, n_chunks=n_chunks),
        grid=(kv, t // tq),
        in_specs=[
            pl.BlockSpec((group, dh, tq), lambda h, i: (h, 0, i)),
            pl.BlockSpec((None, t, dh), lambda h, i: (h, 0, 0)),
            pl.BlockSpec((None, n_chunks, rows, tk), lambda h, i: (h, 0, 0, 0)),
        ],
        out_specs=pl.BlockSpec((tq, group * dh), lambda h, i: (i, h)),
        out_shape=jax.ShapeDtypeStruct((t, heads * dh), BF16),
        scratch_shapes=[pltpu.VMEM((group, tk, tq), F32), pltpu.VMEM((group, tk, tq), F32),
                        pltpu.VMEM((group, 1, tq), F32), pltpu.VMEM((group, 1, tq), F32)],
        compiler_params=_params("parallel", "parallel"),
        name="flash_attention",
    )(q_t, k, v_t)


def _convpool_body(ah_ref, ab_ref, ac_ref, pu_ref, ah_up, ac_up, pu_up, ah_dn, ac_dn, pu_dn,
                   cw_ref, pw_ref, ps_ref, yc_ref, yp_ref, xbuf, pbuf,
                   *, tr, ctx_tiles, n_tiles, ctx_len, seq_len):
    i = pl.program_id(0)
    has_up = jnp.logical_and(i != 0, i != ctx_tiles)
    has_dn = jnp.logical_and(i != ctx_tiles - 1, i != n_tiles - 1)

    f32 = lambda ref: ref[...].astype(F32)
    xbuf[0:HALO] = jnp.where(has_up, f32(ac_up) * f32(ah_up), 0.0)
    xbuf[HALO:HALO + tr] = f32(ac_ref) * f32(ah_ref)
    xbuf[HALO + tr:HALO + tr + HALO] = jnp.where(has_dn, f32(ac_dn) * f32(ah_dn), 0.0)
    conv = (cw_ref[0:1] * xbuf[HALO - 1:HALO - 1 + tr] + cw_ref[1:2] * xbuf[HALO:HALO + tr]
            + cw_ref[2:3] * xbuf[HALO + 1:HALO + 1 + tr])
    yc_ref[...] = (f32(ab_ref) * conv).astype(yc_ref.dtype)

    pbuf[0:HALO] = jnp.where(has_up, f32(pu_up), 0.0)
    pbuf[HALO:HALO + tr] = f32(pu_ref)
    pbuf[HALO + tr:HALO + tr + HALO] = jnp.where(has_dn, f32(pu_dn), 0.0)
    in_ctx = i < ctx_tiles
    pos = lax.broadcasted_iota(jnp.int32, (tr, 1), 0) + jnp.where(in_ctx, i, i - ctx_tiles) * tr
    length = jnp.where(in_ctx, ctx_len, seq_len)
    pg = pw_ref.shape[-1]
    for g, win in enumerate(POOL_WINDOWS):
        cols = slice(g * pg, (g + 1) * pg)
        total = None
        for s in range(-(win // 2), win - win // 2):
            piece = pbuf[HALO + s:HALO + s + tr, cols]
            total = piece if total is None else total + piece
        count = jnp.minimum(pos - win // 2 + win, length) - jnp.maximum(pos - win // 2, 0)
        pooled = total / count.astype(F32) - pbuf[HALO:HALO + tr, cols]
        y = jnp.dot(pooled.astype(BF16), pw_ref[g], preferred_element_type=F32) * ps_ref[:, cols]
        yp_ref[:, cols] = y.astype(yp_ref.dtype)


def conv_pool(p, conv_w, pool_w, pool_scale, layer, *, width, tr, ctx_len, seq_len):
    t = p.shape[0]
    n_tiles = t // tr
    ctx_tiles = ctx_len // tr
    assert ctx_len % tr == 0 and tr % HALO == 0 and max(POOL_WINDOWS) // 2 <= HALO
    hb = tr // HALO
    main = lambda c: pl.BlockSpec((tr, width), lambda i: (i, c))
    up = lambda c: pl.BlockSpec((HALO, width), lambda i: (jnp.maximum(i * hb - 1, 0), c))
    dn = lambda c: pl.BlockSpec((HALO, width), lambda i: (jnp.minimum((i + 1) * hb, t // HALO - 1), c))
    return pl.pallas_call(
        functools.partial(_convpool_body, tr=tr, ctx_tiles=ctx_tiles, n_tiles=n_tiles,
                          ctx_len=ctx_len, seq_len=seq_len),
        grid=(n_tiles,),
        in_specs=[main(0), main(1), main(2), main(4), up(0), up(2), up(4), dn(0), dn(2), dn(4),
                  pl.BlockSpec((None,) + conv_w.shape[1:], lambda i: (layer, 0, 0)),
                  pl.BlockSpec((None,) + pool_w.shape[1:], lambda i: (layer, 0, 0, 0)),
                  pl.BlockSpec((None, 1, width), lambda i: (layer, 0, 0))],
        out_specs=[pl.BlockSpec((tr, width), lambda i: (i, 0))] * 2,
        out_shape=[jax.ShapeDtypeStruct((t, width), BF16)] * 2,
        scratch_shapes=[pltpu.VMEM((tr + 2 * HALO, width), F32)] * 2,
        compiler_params=_params("parallel"),
        name="conv_pool",
    )(p, p, p, p, p, p, p, p, p, p, conv_w, pool_w, pool_scale.reshape(-1, 1, width))


def s5_chunk_weights(lam_re, lam_im, log_step, b_re, b_im, c_re, c_im):
    tc = S5_CHUNK
    lam = lax.complex(lam_re.astype(F32), lam_im.astype(F32))
    z = lam * jnp.exp(log_step.astype(F32))[..., None]
    lam_bar = jnp.exp(z)
    b_bar = ((lam_bar - 1.0) / lam)[..., None] * lax.complex(b_re.astype(F32), b_im.astype(F32))
    c = lax.complex(c_re.astype(F32), c_im.astype(F32))
    depth, _, groups, p_dim, s_dim = b_bar.shape
    powers = jnp.exp(z[..., None, :] * jnp.arange(tc + 1, dtype=F32)[:, None])

    cp = c[..., None, :, :] * powers[..., :tc, None, :]
    kern = (jnp.einsum('ldgtop,ldgpi->ldgtoi', jnp.real(cp), jnp.real(b_bar), precision=HIGHEST)
            - jnp.einsum('ldgtop,ldgpi->ldgtoi', jnp.imag(cp), jnp.imag(b_bar), precision=HIGHEST))
    k_f, k_b = kern[:, 0], kern[:, 1]
    by_lag = jnp.concatenate([k_b[:, :, :0:-1], k_f[:, :, :1] + k_b[:, :, :1], k_f[:, :, 1:],
                              jnp.zeros_like(k_f[:, :, :1])], axis=2)
    skew = jnp.tile(by_lag, (1, 1, tc, 1, 1))[:, :, :tc * (2 * tc - 1)]
    skew = skew.reshape(depth, groups, tc, 2 * tc - 1, s_dim, s_dim)[:, :, :, tc - 1:]
    toeplitz = jnp.transpose(skew, (0, 1, 2, 5, 3, 4)).reshape(depth, groups, tc * s_dim, tc * s_dim)

    f_f = c[:, 0][:, :, None] * powers[:, 0, :, 1:tc + 1, None, :]
    f_b = c[:, 1][:, :, None] * powers[:, 1, :, tc:0:-1, None, :]
    rows = lambda a: jnp.transpose(a, (0, 1, 4, 2, 3)).reshape(depth, groups, p_dim, tc * s_dim)
    w_out = jnp.concatenate([toeplitz, rows(jnp.real(f_f)), rows(jnp.real(f_b)),
                             rows(-jnp.imag(f_f)), rows(-jnp.imag(f_b))], axis=2)

    e_f = powers[:, 0, :, tc - 1::-1][:, :, :tc, :, None] * b_bar[:, 0][:, :, None]
    e_b = powers[:, 1, :, :tc, :, None] * b_bar[:, 1][:, :, None]
    cols = lambda a: jnp.transpose(a, (0, 1, 2, 4, 3)).reshape(depth, groups, tc * s_dim, p_dim)
    w_state = jnp.concatenate([cols(jnp.real(e_f)), cols(jnp.real(e_b)),
                               cols(jnp.imag(e_f)), cols(jnp.imag(e_b))], axis=3)

    last = powers[..., tc, :]
    decay = jnp.stack([jnp.concatenate([jnp.real(last[:, 0]), jnp.real(last[:, 1])], axis=-1),
                       jnp.concatenate([jnp.imag(last[:, 0]), jnp.imag(last[:, 1])], axis=-1)], axis=2)
    return _slab_operators(w_state, w_out, decay, tc, s_dim)


def _slab_operators(w_state, w_out, decay, tc, s_dim):
    depth, groups, cs, st = w_state.shape
    gl = LANES // s_dim
    k = groups // gl
    p_dim = st // 4
    eye = lambda a, b, nd: jnp.eye(gl, dtype=F32).reshape([gl if ax in (a, b) else 1 for ax in range(nd)])
    ws = jnp.transpose(w_state.reshape(depth, k, gl, tc, s_dim, 2, 2, p_dim), (0, 1, 3, 2, 4, 5, 6, 7))
    ws = (ws[..., None, :] * eye(3, 7, 9)).reshape(depth, k, tc * gl * s_dim, 4 * gl * p_dim)
    top = jnp.transpose(w_out[:, :, :cs].reshape(depth, k, gl, tc, s_dim, tc, s_dim), (0, 1, 3, 2, 4, 5, 6))
    top = (top[..., None, :] * eye(3, 6, 8)).reshape(depth, k, tc * gl * s_dim, tc * gl * s_dim)
    bot = jnp.transpose(w_out[:, :, cs:].reshape(depth, k, gl, 2, 2, p_dim, tc, s_dim), (0, 1, 3, 4, 2, 5, 6, 7))
    bot = (bot[..., None, :] * eye(4, 7, 9)).reshape(depth, k, 4 * gl * p_dim, tc * gl * s_dim)
    decay_s = jnp.transpose(decay.reshape(depth, k, gl, 2, 2, p_dim), (0, 1, 3, 4, 2, 5))
    return (ws.astype(BF16), jnp.concatenate([top, bot], axis=2).astype(BF16),
            decay_s.reshape(depth, k, 2, 2 * gl * p_dim))


def _s5_state_body(u_ref, w_ref, decay_ref, h_ref, s_scr, *, n, ctx_chunks):
    half = decay_ref.shape[-1]
    q = half // 2
    rows = n // 2 if n % 16 == 0 else n
    for r in range(0, n, rows):
        s_scr[r:r + rows] = jnp.dot(u_ref[r:r + rows], w_ref[...], preferred_element_type=F32)
    parts = ((slice(0, q), slice(half, half + q)), (slice(q, half), slice(half + q, 2 * half)))
    decays = [(decay_ref[0:1, lo:lo + q], decay_ref[1:2, lo:lo + q]) for lo in (0, q)]

    def scan(j, carry):
        jb = jnp.where(j < ctx_chunks, ctx_chunks - 1 - j, n - 1 + ctx_chunks - j)
        out = []
        for row, (re, im), (a_re, a_im), (c_re, c_im) in zip((j, jb), parts, decays, carry):
            s_re, s_im = s_scr[pl.ds(row, 1), re], s_scr[pl.ds(row, 1), im]
            s_scr[pl.ds(row, 1), re] = c_re
            s_scr[pl.ds(row, 1), im] = c_im
            out.append((a_re * c_re - a_im * c_im + s_re, a_re * c_im + a_im * c_re + s_im))
        return tuple(out)

    zero = jnp.zeros((1, q), F32)
    lax.fori_loop(0, n, scan, ((zero, zero), (zero, zero)))
    h_ref[...] = s_scr[...].astype(h_ref.dtype)


def s5_state(u, w_state_s, decay_s, layer, *, ctx_chunks):
    k, n, cs = u.shape
    st = w_state_s.shape[-1]
    return pl.pallas_call(
        functools.partial(_s5_state_body, n=n, ctx_chunks=ctx_chunks),
        grid=(k,),
        in_specs=[
            pl.BlockSpec((None, n, cs), lambda i: (i, 0, 0)),
            pl.BlockSpec((None, None, cs, st), lambda i: (layer, i, 0, 0)),
            pl.BlockSpec((None, None, 2, st // 2), lambda i: (layer, i, 0, 0)),
        ],
        out_specs=pl.BlockSpec((None, n, st), lambda i: (i, 0, 0)),
        out_shape=jax.ShapeDtypeStruct((k, n, st), BF16),
        scratch_shapes=[pltpu.VMEM((n, st), F32)],
        compiler_params=_params("parallel"),
        name="s5_state",
    )(u, w_state_s, decay_s)


def _s5_readout_body(u_ref, h_ref, w_ref, y_ref):
    cs = u_ref.shape[-1]
    y_ref[...] = (jnp.dot(u_ref[...], w_ref[:cs], preferred_element_type=F32)
                  + jnp.dot(h_ref[...], w_ref[cs:], preferred_element_type=F32))


def s5_readout(u, h, w_out_s, layer, *, tn):
    k, n, cs = u.shape
    st = h.shape[-1]
    return pl.pallas_call(
        _s5_readout_body,
        grid=(k, cs // tn),
        in_specs=[
            pl.BlockSpec((None, n, cs), lambda i, j: (i, 0, 0)),
            pl.BlockSpec((None, n, st), lambda i, j: (i, 0, 0)),
            pl.BlockSpec((None, None, cs + st, tn), lambda i, j: (layer, i, 0, j)),
        ],
        out_specs=pl.BlockSpec((None, n, tn), lambda i, j: (i, 0, j)),
        out_shape=jax.ShapeDtypeStruct((k, n, cs), F32),
        compiler_params=_params("parallel", "parallel"),
        name="s5_readout",
    )(u, h, w_out_s)


def _s5_glu_body(y_ref, u_ref, d_ref, w_ref, o_ref):
    z = jax.nn.gelu(y_ref[...] + d_ref[...] * u_ref[...].astype(F32))
    gate = _sigmoid(jnp.dot(z.astype(BF16), w_ref[...], preferred_element_type=F32))
    o_ref[...] = (z * gate).astype(o_ref.dtype)


def s5_glu(y, p, d, w_glu, layer, *, width, tr):
    t = p.shape[0]
    return pl.pallas_call(
        _s5_glu_body,
        grid=(t // tr,),
        in_specs=[
            pl.BlockSpec((tr, width), lambda i: (i, 0)),
            pl.BlockSpec((tr, width), lambda i: (i, 3)),
            pl.BlockSpec((None, 1, width), lambda i: (layer, 0, 0)),
            pl.BlockSpec((None, width, width), lambda i: (layer, 0, 0)),
        ],
        out_specs=pl.BlockSpec((tr, width), lambda i: (i, 0)),
        out_shape=jax.ShapeDtypeStruct((t, width), BF16),
        compiler_params=_params("parallel"),
        name="s5_glu",
    )(y, p, d.reshape(-1, 1, width), w_glu)


def _merge_body(gl_ref, yc_ref, ys_ref, yp_ref, ya_ref, gu_ref, gb_ref, wc_ref, ws_ref, wp_ref, wa_ref,
                o_ref):
    g_low = gl_ref[...]
    total = None
    for b, (y_ref, w_ref) in enumerate(((yc_ref, wc_ref), (ys_ref, ws_ref), (yp_ref, wp_ref),
                                        (ya_ref, wa_ref))):
        gate = _sigmoid(jnp.dot(g_low, gu_ref[b], preferred_element_type=F32) + gb_ref[b])
        term = gate * jnp.dot(y_ref[...], w_ref[...], preferred_element_type=F32)
        total = term if total is None else total + term
    o_ref[...] = total.astype(o_ref.dtype)


def merge(p, ys, gate_up, gate_bias, w_outs, layer, *, gate_col, tm, tn):
    t = p.shape[0]
    _, nb, rank, d = gate_up.shape
    row = lambda a: pl.BlockSpec((tm, a.shape[1]), lambda i, j: (i, 0))
    col = lambda a: pl.BlockSpec((None, a.shape[1], tn), lambda i, j: (layer, 0, j))
    return pl.pallas_call(
        _merge_body,
        grid=(t // tm, d // tn),
        in_specs=[pl.BlockSpec((tm, rank), lambda i, j: (i, gate_col))] + [row(y) for y in ys]
                 + [pl.BlockSpec((None, nb, rank, tn), lambda i, j: (layer, 0, 0, j)),
                    pl.BlockSpec((None, nb, 1, tn), lambda i, j: (layer, 0, 0, j))] + [col(w) for w in w_outs],
        out_specs=pl.BlockSpec((tm, tn), lambda i, j: (i, j)),
        out_shape=jax.ShapeDtypeStruct((t, d), BF16),
        compiler_params=_params("parallel", "parallel"),
        name="merge",
    )(p, *ys, gate_up, gate_bias.reshape(-1, nb, 1, d), *w_outs)


def _rope_tables(ctx_len, seq_len, dh):
    pairs = dh // 4
    rows = seq_len // GRID_W
    row = jnp.repeat(jnp.arange(rows), GRID_W).astype(F32)
    col = jnp.tile(jnp.arange(GRID_W), rows).astype(F32)
    inv = ROPE_THETA ** (-jnp.arange(pairs, dtype=F32) / pairs)
    ang = jnp.concatenate([row[:, None] * inv, col[:, None] * inv], axis=-1)
    cos = jnp.repeat(jnp.cos(ang), 2, axis=-1)
    sin = jnp.stack([-jnp.sin(ang), jnp.sin(ang)], axis=-1).reshape(seq_len, dh)
    cos = jnp.concatenate([jnp.ones((ctx_len, dh), F32), cos], axis=0)
    sin = jnp.concatenate([jnp.zeros((ctx_len, dh), F32), sin], axis=0)
    return cos, sin


def kernel(x, c, ctx, c_ctx, ada_down, ada_up, ada_bias, norm_gains, w_in, conv_w, s5_lambda_re, s5_lambda_im, s5_log_step, s5_b_re, s5_b_im, s5_c_re, s5_c_im, s5_d, s5_w_glu, pool_w, pool_scale, qk_norm, gate_up, gate_bias, w_out_conv, w_out_s5, w_out_pool, w_out_attn, w_o, ffn_w_gate, ffn_w_up, ffn_w_down):
    batch, seq_len, d = x.shape
    ctx_len = ctx.shape[1]
    depth = w_in.shape[0]
    assert batch == 1 and c.shape[0] == 1
    t = ctx_len + seq_len
    width = conv_w.shape[-1]
    assert s5_d.shape[-1] == width and pool_scale.shape[-1] == width
    dh = qk_norm.shape[-1]
    q_width = w_out_attn.shape[1]
    rank = gate_up.shape[2]
    in_width = w_in.shape[-1]
    kv_width = (in_width - 5 * width - q_width - rank) // 2
    heads, kv_heads = q_width // dh, kv_width // dh
    groups, s_dim = s5_b_re.shape[2], s5_b_re.shape[-1]
    slabs = width // LANES
    assert groups * s_dim == width and LANES % s_dim == 0
    hidden = ffn_w_gate.shape[-1]
    q_col, k_col, v_col = 5 * width, 5 * width + q_width, 5 * width + q_width + kv_width
    gate_col = (v_col + kv_width) // rank
    assert q_col % dh == 0 and (v_col + kv_width) % rank == 0 and width % dh == 0

    tr = _tile(ctx_len, 256, 8)
    tm = _tile(t, 1280, 256)
    tm_half = _tile(t, 640, 128)
    tk_att = _tile(t, 1280, 256)
    tq = _tile(ctx_len, 256, 128)
    ctx_tiles = ctx_len // tr

    w_in_b, w_o_b, w_gate_b, w_up_b, w_down_b = (w.astype(BF16) for w in (w_in, w_o, ffn_w_gate, ffn_w_up,
                                                                         ffn_w_down))
    w_outs_b = [w.astype(BF16) for w in (w_out_conv, w_out_s5, w_out_pool, w_out_attn)]
    gate_up_b, pool_w_b, w_glu_b = gate_up.astype(BF16), pool_w.astype(BF16), s5_w_glu.astype(BF16)
    w_state, w_s5out, decay = s5_chunk_weights(s5_lambda_re, s5_lambda_im, s5_log_step,
                                               s5_b_re, s5_b_im, s5_c_re, s5_c_im)
    cos, sin = _rope_tables(ctx_len, seq_len, dh)
    qk_gains = qk_norm.reshape(depth, 2, 1, dh)

    cond = jnp.concatenate([c_ctx[None, :], c, jnp.zeros((6, d), F32)], axis=0)
    mods = ada_modulation(cond, ada_down, ada_up, ada_bias)
    mods = mods[:, :2].reshape(depth, 2, N_MOD, d)
    mods = jnp.concatenate([mods, jnp.zeros((depth, 2, 8 - N_MOD, d), F32)], axis=2)

    stream = jnp.concatenate([ctx[0], x[0]], axis=0)
    row_kw = dict(ctx_tiles=ctx_tiles, tr=tr)
    h = modulate(stream, mods, norm_gains, 0, rows=(0, 1, 0), **row_kw)
    n_chunks = t // S5_CHUNK
    for l in range(depth):
        p = matmul(h, w_in_b, l, tm=tm, tn=_tile(in_width, 512, 256), out_dtype=BF16)

        q_t = qk_prep(p, qk_gains, cos, sin, l, 0, col0=q_col // dh, heads=heads, tr=tk_att,
                      scale=dh ** -0.5 * math.log2(math.e), transpose=True)
        k_n = qk_prep(p, qk_gains, cos, sin, l, 1, col0=k_col // dh, heads=kv_heads, tr=tk_att,
                      scale=1.0, transpose=False)
        v_t = v_prep(p, col0=v_col // dh, heads=kv_heads, dh=dh, tk=tk_att)
        y_attn = flash_attention(q_t, k_n, v_t, tq=tq, ctx_len=ctx_len)

        y_conv, y_pool = conv_pool(p, conv_w, pool_w_b, pool_scale, l, width=width, tr=tr,
                                   ctx_len=ctx_len, seq_len=seq_len)

        u = p[:, 3 * width:4 * width].reshape(n_chunks, S5_CHUNK, slabs, LANES)
        u = jnp.transpose(u, (2, 0, 1, 3)).reshape(slabs, n_chunks, S5_CHUNK * LANES)
        h_s5 = s5_state(u, w_state, decay, l, ctx_chunks=ctx_len // S5_CHUNK)
        y_s5 = s5_readout(u, h_s5, w_s5out, l, tn=_tile(S5_CHUNK * LANES, 1024, 256))
        y_s5 = jnp.transpose(y_s5.reshape(slabs, n_chunks, S5_CHUNK, LANES), (1, 2, 0, 3)).reshape(t, width)
        y_s5 = s5_glu(y_s5, p, s5_d, w_glu_b, l, width=width, tr=tm_half)

        merged = merge(p, [y_conv, y_s5, y_pool, y_attn], gate_up_b, gate_bias, w_outs_b, l,
                       gate_col=gate_col, tm=tm_half, tn=_tile(d, 512, 256))
        mix = matmul(merged, w_o_b, l, tm=tm, tn=_tile(d, 512, 256), out_dtype=BF16)
        stream, h = residual(stream, mix, mods, norm_gains, l, gate_row=2, post_row=1, nxt=(3, 4, 2), **row_kw)

        hid = ffn_up(h, w_gate_b, w_up_b, l, tm=tm, tn=_tile(hidden, 256, 128))
        ffn = matmul(hid, w_down_b, l, tm=tm, tn=_tile(d, 512, 256), nk=2)
        if l + 1 < depth:
            stream, h = residual(stream, ffn, mods, norm_gains, l, gate_row=5, post_row=3, nxt=(0, 1, 0),
                                 next_layer=l + 1, **row_kw)
        else:
            stream, _ = residual(stream, ffn, mods, norm_gains, l, gate_row=5, post_row=3,
                                 first_tile=ctx_tiles, **row_kw)
    return stream[None]
```

```python
import functools
import math

import jax
import jax.numpy as jnp
import numpy as np
from jax import lax
from jax.experimental import pallas as pl
from jax.experimental.pallas import tpu as pltpu

EPS = 1e-6
GRID_W = 64
ROPE_THETA = 10000.0
POOL_WINDOWS = (2, 4, 8, 16)
N_MOD = 6
S5_CHUNK = 16
HALO = 16
LANES = 128
ONES_ROWS = 16
NEG_BIG = -1e30
V7X_VMEM_LIMIT = 56 * 1024 * 1024
F32 = jnp.float32
BF16 = jnp.bfloat16
HIGHEST = lax.Precision.HIGHEST


def _params(*sem):
    return pltpu.CompilerParams(dimension_semantics=sem, vmem_limit_bytes=V7X_VMEM_LIMIT)


def _tile(n, target, mult):
    best = None
    for t in range(mult, min(n, target) + 1, mult):
        if n % t == 0:
            best = t
    assert best is not None, (n, target, mult)
    return best


def _rms(x, gain):
    return x * lax.rsqrt(jnp.mean(x * x, axis=-1, keepdims=True) + EPS) * gain


def _sigmoid(x):
    return 1.0 / (1.0 + jnp.exp(-x))


def _ada_body(cond_ref, down_ref, up_ref, bias_ref, o_ref):
    s = cond_ref[...]
    s = s * _sigmoid(s)
    low = jnp.dot(s, down_ref[...], precision=HIGHEST, preferred_element_type=F32)
    o_ref[...] = jnp.dot(low, up_ref[...], precision=HIGHEST, preferred_element_type=F32) + bias_ref[...]


def ada_modulation(cond, down, up, bias):
    depth, d, rank = down.shape
    width = up.shape[-1]
    tn = _tile(width, 4096, 128)
    return pl.pallas_call(
        _ada_body,
        grid=(depth, width // tn),
        in_specs=[
            pl.BlockSpec((8, d), lambda l, j: (0, 0)),
            pl.BlockSpec((None, d, rank), lambda l, j: (l, 0, 0)),
            pl.BlockSpec((None, rank, tn), lambda l, j: (l, 0, j)),
            pl.BlockSpec((None, 1, tn), lambda l, j: (l, 0, j)),
        ],
        out_specs=pl.BlockSpec((None, 8, tn), lambda l, j: (l, 0, j)),
        out_shape=jax.ShapeDtypeStruct((depth, 8, width), F32),
        compiler_params=_params("arbitrary", "arbitrary"),
        name="ada_modulation",
    )(cond, down, up, bias.reshape(depth, 1, width))


def _modulated(x, mod, gains, rows):
    shift_row, scale_row, gain_row = rows
    h = _rms(x, gains[gain_row:gain_row + 1])
    return h * (1.0 + mod[scale_row:scale_row + 1]) + mod[shift_row:shift_row + 1]


def _modulate_body(x_ref, mod_ref, gain_ref, h_ref, *, rows):
    h_ref[...] = _modulated(x_ref[...], mod_ref[...], gain_ref[...], rows).astype(h_ref.dtype)


def modulate(x, mods, gains, layer, *, ctx_tiles, tr, rows):
    t, d = x.shape
    return pl.pallas_call(
        functools.partial(_modulate_body, rows=rows),
        grid=(t // tr,),
        in_specs=[
            pl.BlockSpec((tr, d), lambda i: (i, 0)),
            pl.BlockSpec((None, None, 8, d), lambda i: (layer, jnp.where(i < ctx_tiles, 0, 1), 0, 0)),
            pl.BlockSpec((None,) + gains.shape[1:], lambda i: (layer, 0, 0)),
        ],
        out_specs=pl.BlockSpec((tr, d), lambda i: (i, 0)),
        out_shape=jax.ShapeDtypeStruct((t, d), BF16),
        compiler_params=_params("parallel"),
        name="modulate",
    )(x, mods, gains)


def _residual_body(x_ref, y_ref, mod_ref, gain_ref, modn_ref, gainn_ref, xo_ref, *h_ref,
                   gate_row, post_row, nxt):
    mod = mod_ref[...]
    xn = x_ref[...] + mod[gate_row:gate_row + 1] * _rms(y_ref[...].astype(F32), gain_ref[post_row:post_row + 1])
    xo_ref[...] = xn
    if nxt is not None:
        h_ref[0][...] = _modulated(xn, modn_ref[...], gainn_ref[...], nxt).astype(h_ref[0].dtype)


def residual(x, y, mods, gains, layer, *, ctx_tiles, tr, gate_row, post_row, nxt=None, next_layer=None,
             first_tile=0):
    t, d = x.shape
    n_tiles = t // tr - first_tile
    if next_layer is None:
        next_layer = layer
    who = lambda i: jnp.where(i + first_tile < ctx_tiles, 0, 1)
    row_in = pl.BlockSpec((tr, d), lambda i: (i + first_tile, 0))
    row_out = pl.BlockSpec((tr, d), lambda i: (i, 0))
    mod_spec = lambda l: pl.BlockSpec((None, None, 8, d), lambda i: (l, who(i), 0, 0))
    gain_spec = lambda l: pl.BlockSpec((None,) + gains.shape[1:], lambda i: (l, 0, 0))
    out_specs = [row_out]
    out_shape = [jax.ShapeDtypeStruct((n_tiles * tr, d), F32)]
    if nxt is not None:
        out_specs.append(row_out)
        out_shape.append(jax.ShapeDtypeStruct((n_tiles * tr, d), BF16))
    out = pl.pallas_call(
        functools.partial(_residual_body, gate_row=gate_row, post_row=post_row, nxt=nxt),
        grid=(n_tiles,),
        in_specs=[row_in, row_in, mod_spec(layer), gain_spec(layer), mod_spec(next_layer),
                  gain_spec(next_layer)],
        out_specs=out_specs,
        out_shape=out_shape,
        compiler_params=_params("parallel"),
        name="residual_norm",
    )(x, y, mods, gains, mods, gains)
    return out if nxt is not None else (out[0], None)


def _matmul_body(a_ref, b_ref, o_ref, *, nk):
    r = jnp.dot(a_ref[...], b_ref[...], preferred_element_type=F32)
    if nk == 1:
        o_ref[...] = r.astype(o_ref.dtype)
    else:
        k = pl.program_id(2)

        @pl.when(k == 0)
        def _():
            o_ref[...] = r

        @pl.when(k > 0)
        def _():
            o_ref[...] += r


def matmul(a, b, layer, *, tm, tn, nk=1, out_dtype=F32):
    m, k = a.shape
    n = b.shape[-1]
    assert k % nk == 0 and (nk == 1 or out_dtype == F32)
    tk = k // nk
    return pl.pallas_call(
        functools.partial(_matmul_body, nk=nk),
        grid=(m // tm, n // tn, nk),
        in_specs=[
            pl.BlockSpec((tm, tk), lambda i, j, kk: (i, kk)),
            pl.BlockSpec((None, tk, tn), lambda i, j, kk: (layer, kk, j)),
        ],
        out_specs=pl.BlockSpec((tm, tn), lambda i, j, kk: (i, j)),
        out_shape=jax.ShapeDtypeStruct((m, n), out_dtype),
        compiler_params=_params("parallel", "parallel", "arbitrary"),
        name="matmul",
    )(a, b)


def _ffn_up_body(a_ref, wg_ref, wu_ref, o_ref):
    a = a_ref[...]
    g = jnp.dot(a, wg_ref[...], preferred_element_type=F32)
    u = jnp.dot(a, wu_ref[...], preferred_element_type=F32)
    o_ref[...] = (g * _sigmoid(g) * u).astype(o_ref.dtype)


def ffn_up(a, w_gate, w_up, layer, *, tm, tn):
    m, k = a.shape
    n = w_gate.shape[-1]
    w_spec = pl.BlockSpec((None, k, tn), lambda i, j: (layer, 0, j))
    return pl.pallas_call(
        _ffn_up_body,
        grid=(m // tm, n // tn),
        in_specs=[pl.BlockSpec((tm, k), lambda i, j: (i, 0)), w_spec, w_spec],
        out_specs=pl.BlockSpec((tm, tn), lambda i, j: (i, j)),
        out_shape=jax.ShapeDtypeStruct((m, n), BF16),
        compiler_params=_params("parallel", "parallel"),
        name="ffn_up",
    )(a, w_gate, w_up)


def _qk_prep_body(x_ref, gain_ref, cos_ref, sin_ref, o_ref, *, scale, transpose):
    y = _rms(x_ref[...].astype(F32), gain_ref[...])
    lanes = y.shape[-1]
    even = lax.broadcasted_iota(jnp.int32, (1, lanes), 1) % 2 == 0
    partner = jnp.where(even, pltpu.roll(y, lanes - 1, 1), pltpu.roll(y, 1, 1))
    y = (y * cos_ref[...] + partner * sin_ref[...]) * scale
    if transpose:
        o_ref[...] = y.T.astype(o_ref.dtype)
    else:
        o_ref[...] = y.astype(o_ref.dtype)


def qk_prep(p, gains, cos, sin, layer, which, *, col0, heads, tr, scale, transpose):
    t = p.shape[0]
    dh = gains.shape[-1]
    if transpose:
        out_spec = pl.BlockSpec((None, dh, tr), lambda i, h: (h, 0, i))
        out_shape = jax.ShapeDtypeStruct((heads, dh, t), BF16)
    else:
        out_spec = pl.BlockSpec((None, tr, dh), lambda i, h: (h, i, 0))
        out_shape = jax.ShapeDtypeStruct((heads, t, dh), BF16)
    return pl.pallas_call(
        functools.partial(_qk_prep_body, scale=scale, transpose=transpose),
        grid=(t // tr, heads),
        in_specs=[
            pl.BlockSpec((tr, dh), lambda i, h: (i, col0 + h)),
            pl.BlockSpec((None, None, 1, dh), lambda i, h: (layer, which, 0, 0)),
            pl.BlockSpec((tr, dh), lambda i, h: (i, 0)),
            pl.BlockSpec((tr, dh), lambda i, h: (i, 0)),
        ],
        out_specs=out_spec,
        out_shape=out_shape,
        compiler_params=_params("parallel", "parallel"),
        name="qk_prep",
    )(p, gains, cos, sin)


def _v_prep_body(x_ref, o_ref):
    dh = x_ref.shape[1]
    o_ref[:dh] = x_ref[...].astype(F32).T.astype(o_ref.dtype)
    first = lax.broadcasted_iota(jnp.int32, (ONES_ROWS, o_ref.shape[1]), 0) == 0
    o_ref[dh:] = jnp.where(first, 1.0, 0.0).astype(o_ref.dtype)


def v_prep(p, *, col0, heads, dh, tk):
    t = p.shape[0]
    return pl.pallas_call(
        _v_prep_body,
        grid=(t // tk, heads),
        in_specs=[pl.BlockSpec((tk, dh), lambda i, h: (i, col0 + h))],
        out_specs=pl.BlockSpec((None, None, dh + ONES_ROWS, tk), lambda i, h: (h, i, 0, 0)),
        out_shape=jax.ShapeDtypeStruct((heads, t // tk, dh + ONES_ROWS, tk), BF16),
        compiler_params=_params("parallel", "parallel"),
        name="v_prep",
    )(p)


def _flash_body(q_ref, k_ref, v_ref, o_ref, s_a, s_b, m_a, m_b,
                *, tq, tk, group, ctx_len, ctx_q_tiles, n_chunks):
    dh = q_ref.shape[1]
    rows = v_ref.shape[1]

    def scores(c, g, s_ref, m_ref, masked):
        k_c = k_ref[pl.ds(pl.multiple_of(c * tk, tk), tk), :]
        s = jnp.dot(k_c, q_ref[g], preferred_element_type=F32)
        if masked:
            key = c * tk + lax.broadcasted_iota(jnp.int32, (tk, 1), 0)
            s = jnp.where(key < ctx_len, s, NEG_BIG)
        s_ref[g] = s
        m_ref[g] = jnp.max(s, axis=0, keepdims=True)

    def absorb(c, g, s_ref, m_ref, carry):
        m, acc = carry
        m_new = jnp.maximum(m, m_ref[g])
        p = jnp.exp2(s_ref[g] - m_new).astype(BF16)
        acc = jnp.exp2(m - m_new) * acc + jnp.dot(v_ref[c], p, preferred_element_type=F32)
        return m_new, acc

    def run(chunks, masked):
        carry = tuple((jnp.full((1, tq), NEG_BIG, F32), jnp.zeros((rows, tq), F32)) for _ in range(group))
        for g in range(group):
            scores(0, g, s_a, m_a, masked)

        def half(c_next, c_cur, nxt, cur, carry):
            out = []
            for g in range(group):
                scores(c_next, g, *nxt, masked)
                out.append(absorb(c_cur, g, *cur, carry[g]))
            return tuple(out)

        def pair(c, carry):
            carry = half(c + 1, c, (s_b, m_b), (s_a, m_a), carry)
            return half(jnp.minimum(c + 2, chunks - 1), c + 1, (s_a, m_a), (s_b, m_b), carry)

        def quad(i, carry):
            return pair(4 * i + 2, pair(4 * i, carry))

        carry = lax.fori_loop(0, chunks // 4, quad, carry)
        if chunks % 4 >= 2:
            carry = pair(chunks // 4 * 4, carry)
        for g in range(group):
            m, acc = carry[g]
            if chunks % 2 == 1:
                m, acc = absorb(chunks - 1, g, s_a, m_a, (m, acc))
            o = acc[:dh] / acc[dh:dh + 1]
            o_ref[:, g * dh:(g + 1) * dh] = o.T.astype(o_ref.dtype)

    i = pl.program_id(1)

    @pl.when(i < ctx_q_tiles)
    def _():
        run(-(-ctx_len // tk), True)

    @pl.when(i >= ctx_q_tiles)
    def _():
        run(n_chunks, False)


def flash_attention(q_t, k, v_t, *, tq, ctx_len):
    heads, dh, t = q_t.shape
    kv, n_chunks, rows, tk = v_t.shape
    group = heads // kv
    assert ctx_len % tq == 0
    return pl.pallas_call(
        functools.partial(_flash_body, tq=tq, tk=tk, group=group, ctx_len=ctx_len,
                          ctx_q_tiles=ctx_len // tq, n_chunks=n_chunks),
        grid=(kv, t // tq),
        in_specs=[
            pl.BlockSpec((group, dh, tq), lambda h, i: (h, 0, i)),
            pl.BlockSpec((None, t, dh), lambda h, i: (h, 0, 0)),
            pl.BlockSpec((None, n_chunks, rows, tk), lambda h, i: (h, 0, 0, 0)),
        ],
        out_specs=pl.BlockSpec((tq, group * dh), lambda h, i: (i, h)),
        out_shape=jax.ShapeDtypeStruct((t, heads * dh), BF16),
        scratch_shapes=[pltpu.VMEM((group, tk, tq), F32), pltpu.VMEM((group, tk, tq), F32),
                        pltpu.VMEM((group, 1, tq), F32), pltpu.VMEM((group, 1, tq), F32)],
        compiler_params=_params("parallel", "parallel"),
        name="flash_attention",
    )(q_t, k, v_t)


def _convpool_body(ah_ref, ab_ref, ac_ref, pu_ref, ah_up, ac_up, pu_up, ah_dn, ac_dn, pu_dn,
                   cw_ref, pw_ref, ps_ref, yc_ref, yp_ref, xbuf, pbuf,
                   *, tr, ctx_tiles, n_tiles, ctx_len, seq_len):
    i = pl.program_id(0)
    has_up = jnp.logical_and(i != 0, i != ctx_tiles)
    has_dn = jnp.logical_and(i != ctx_tiles - 1, i != n_tiles - 1)

    f32 = lambda ref: ref[...].astype(F32)
    xbuf[0:HALO] = jnp.where(has_up, f32(ac_up) * f32(ah_up), 0.0)
    xbuf[HALO:HALO + tr] = f32(ac_ref) * f32(ah_ref)
    xbuf[HALO + tr:HALO + tr + HALO] = jnp.where(has_dn, f32(ac_dn) * f32(ah_dn), 0.0)
    conv = (cw_ref[0:1] * xbuf[HALO - 1:HALO - 1 + tr] + cw_ref[1:2] * xbuf[HALO:HALO + tr]
            + cw_ref[2:3] * xbuf[HALO + 1:HALO + 1 + tr])
    yc_ref[...] = (f32(ab_ref) * conv).astype(yc_ref.dtype)

    pbuf[0:HALO] = jnp.where(has_up, f32(pu_up), 0.0)
    pbuf[HALO:HALO + tr] = f32(pu_ref)
    pbuf[HALO + tr:HALO + tr + HALO] = jnp.where(has_dn, f32(pu_dn), 0.0)
    in_ctx = i < ctx_tiles
    pos = lax.broadcasted_iota(jnp.int32, (tr, 1), 0) + jnp.where(in_ctx, i, i - ctx_tiles) * tr
    length = jnp.where(in_ctx, ctx_len, seq_len)
    pg = pw_ref.shape[-1]
    for g, win in enumerate(POOL_WINDOWS):
        cols = slice(g * pg, (g + 1) * pg)
        total = None
        for s in range(-(win // 2), win - win // 2):
            piece = pbuf[HALO + s:HALO + s + tr, cols]
            total = piece if total is None else total + piece
        count = jnp.minimum(pos - win // 2 + win, length) - jnp.maximum(pos - win // 2, 0)
        pooled = total / count.astype(F32) - pbuf[HALO:HALO + tr, cols]
        y = jnp.dot(pooled.astype(BF16), pw_ref[g], preferred_element_type=F32) * ps_ref[:, cols]
        yp_ref[:, cols] = y.astype(yp_ref.dtype)


def conv_pool(p, conv_w, pool_w, pool_scale, layer, *, width, tr, ctx_len, seq_len):
    t = p.shape[0]
    n_tiles = t // tr
    ctx_tiles = ctx_len // tr
    assert ctx_len % tr == 0 and tr % HALO == 0 and max(POOL_WINDOWS) // 2 <= HALO
    hb = tr // HALO
    main = lambda c: pl.BlockSpec((tr, width), lambda i: (i, c))
    up = lambda c: pl.BlockSpec((HALO, width), lambda i: (jnp.maximum(i * hb - 1, 0), c))
    dn = lambda c: pl.BlockSpec((HALO, width), lambda i: (jnp.minimum((i + 1) * hb, t // HALO - 1), c))
    return pl.pallas_call(
        functools.partial(_convpool_body, tr=tr, ctx_tiles=ctx_tiles, n_tiles=n_tiles,
                          ctx_len=ctx_len, seq_len=seq_len),
        grid=(n_tiles,),
        in_specs=[main(0), main(1), main(2), main(4), up(0), up(2), up(4), dn(0), dn(2), dn(4),
                  pl.BlockSpec((None,) + conv_w.shape[1:], lambda i: (layer, 0, 0)),
                  pl.BlockSpec((None,) + pool_w.shape[1:], lambda i: (layer, 0, 0, 0)),
                  pl.BlockSpec((None, 1, width), lambda i: (layer, 0, 0))],
        out_specs=[pl.BlockSpec((tr, width), lambda i: (i, 0))] * 2,
        out_shape=[jax.ShapeDtypeStruct((t, width), BF16)] * 2,
        scratch_shapes=[pltpu.VMEM((tr + 2 * HALO, width), F32)] * 2,
        compiler_params=_params("parallel"),
        name="conv_pool",
    )(p, p, p, p, p, p, p, p, p, p, conv_w, pool_w, pool_scale.reshape(-1, 1, width))


def s5_chunk_weights(lam_re, lam_im, log_step, b_re, b_im, c_re, c_im):
    tc = S5_CHUNK
    lam = lax.complex(lam_re.astype(F32), lam_im.astype(F32))
    z = lam * jnp.exp(log_step.astype(F32))[..., None]
    lam_bar = jnp.exp(z)
    b_bar = ((lam_bar - 1.0) / lam)[..., None] * lax.complex(b_re.astype(F32), b_im.astype(F32))
    c = lax.complex(c_re.astype(F32), c_im.astype(F32))
    depth, _, groups, p_dim, s_dim = b_bar.shape
    powers = jnp.exp(z[..., None, :] * jnp.arange(tc + 1, dtype=F32)[:, None])

    cp = c[..., None, :, :] * powers[..., :tc, None, :]
    kern = (jnp.einsum('ldgtop,ldgpi->ldgtoi', jnp.real(cp), jnp.real(b_bar), precision=HIGHEST)
            - jnp.einsum('ldgtop,ldgpi->ldgtoi', jnp.imag(cp), jnp.imag(b_bar), precision=HIGHEST))
    k_f, k_b = kern[:, 0], kern[:, 1]
    by_lag = jnp.concatenate([k_b[:, :, :0:-1], k_f[:, :, :1] + k_b[:, :, :1], k_f[:, :, 1:],
                              jnp.zeros_like(k_f[:, :, :1])], axis=2)
    skew = jnp.tile(by_lag, (1, 1, tc, 1, 1))[:, :, :tc * (2 * tc - 1)]
    skew = skew.reshape(depth, groups, tc, 2 * tc - 1, s_dim, s_dim)[:, :, :, tc - 1:]
    toeplitz = jnp.transpose(skew, (0, 1, 2, 5, 3, 4)).reshape(depth, groups, tc * s_dim, tc * s_dim)

    f_f = c[:, 0][:, :, None] * powers[:, 0, :, 1:tc + 1, None, :]
    f_b = c[:, 1][:, :, None] * powers[:, 1, :, tc:0:-1, None, :]
    rows = lambda a: jnp.transpose(a, (0, 1, 4, 2, 3)).reshape(depth, groups, p_dim, tc * s_dim)
    w_out = jnp.concatenate([toeplitz, rows(jnp.real(f_f)), rows(jnp.real(f_b)),
                             rows(-jnp.imag(f_f)), rows(-jnp.imag(f_b))], axis=2)

    e_f = powers[:, 0, :, tc - 1::-1][:, :, :tc, :, None] * b_bar[:, 0][:, :, None]
    e_b = powers[:, 1, :, :tc, :, None] * b_bar[:, 1][:, :, None]
    cols = lambda a: jnp.transpose(a, (0, 1, 2, 4, 3)).reshape(depth, groups, tc * s_dim, p_dim)
    w_state = jnp.concatenate([cols(jnp.real(e_f)), cols(jnp.real(e_b)),
                               cols(jnp.imag(e_f)), cols(jnp.imag(e_b))], axis=3)

    last = powers[..., tc, :]
    decay = jnp.stack([jnp.concatenate([jnp.real(last[:, 0]), jnp.real(last[:, 1])], axis=-1),
                       jnp.concatenate([jnp.imag(last[:, 0]), jnp.imag(last[:, 1])], axis=-1)], axis=2)
    return _slab_operators(w_state, w_out, decay, tc, s_dim)


def _slab_operators(w_state, w_out, decay, tc, s_dim):
    depth, groups, cs, st = w_state.shape
    gl = LANES // s_dim
    k = groups // gl
    p_dim = st // 4
    w_state, w_out = lax.optimization_barrier((w_state.astype(BF16), w_out.astype(BF16)))

    def stacked_rows(a, piece):
        rows, cols = a.shape[2:]
        a = a.reshape(depth, k, gl, rows // piece, piece, cols)
        return jnp.transpose(a, (0, 1, 3, 2, 4, 5)).reshape(depth, k, gl * rows, cols).astype(BF16)

    def spread(cols, piece):
        c = np.arange(gl * cols)
        src = c // (gl * piece) * piece + c % piece
        return jnp.asarray(np.arange(cols)[:, None] == src[None, :], dtype=BF16)

    ws = expand_block_diagonal(stacked_rows(w_state, s_dim), spread(st, p_dim), s_dim, p_dim, gl)
    top = expand_block_diagonal(stacked_rows(w_out[:, :, :cs], s_dim), spread(cs, s_dim), s_dim, s_dim, gl)
    bot = expand_block_diagonal(stacked_rows(w_out[:, :, cs:], p_dim), spread(cs, s_dim), p_dim, s_dim, gl)
    decay_s = jnp.transpose(decay.reshape(depth, k, gl, 2, 2, p_dim), (0, 1, 3, 4, 2, 5))
    return ws, top, bot, decay_s.reshape(depth, k, 2, 2 * gl * p_dim)


def _expand_body(a_ref, c_ref, o_ref, *, row_piece, col_piece, gl):
    r = jnp.dot(a_ref[...], c_ref[...], preferred_element_type=F32)
    row0 = pl.program_id(2) * a_ref.shape[0]
    row_group = (row0 + lax.broadcasted_iota(jnp.int32, (a_ref.shape[0], 1), 0)) // row_piece % gl
    col_group = lax.broadcasted_iota(jnp.int32, (1, c_ref.shape[1]), 1) // col_piece % gl
    o_ref[...] = jnp.where(row_group == col_group, r, 0.0).astype(o_ref.dtype)


def expand_block_diagonal(a, spread, row_piece, col_piece, gl):
    depth, k, rows, cols = a.shape
    tr = _tile(rows, 512, 16)
    return pl.pallas_call(
        functools.partial(_expand_body, row_piece=row_piece, col_piece=col_piece, gl=gl),
        grid=(depth, k, rows // tr),
        in_specs=[pl.BlockSpec((None, None, tr, cols), lambda l, i, r: (l, i, r, 0)),
                  pl.BlockSpec(spread.shape, lambda l, i, r: (0, 0))],
        out_specs=pl.BlockSpec((None, None, tr, gl * cols), lambda l, i, r: (l, i, r, 0)),
        out_shape=jax.ShapeDtypeStruct((depth, k, rows, gl * cols), BF16),
        compiler_params=_params("parallel", "parallel", "parallel"),
        name="expand_block_diagonal",
    )(a, spread)


def _s5_state_body(u_ref, w_ref, decay_ref, h_ref, s_scr, *, n, ctx_chunks):
    half = decay_ref.shape[-1]
    q = half // 2
    rows = n // 2 if n % 16 == 0 else n
    for r in range(0, n, rows):
        s_scr[r:r + rows] = jnp.dot(u_ref[r:r + rows], w_ref[...], preferred_element_type=F32)
    parts = ((slice(0, q), slice(half, half + q)), (slice(q, half), slice(half + q, 2 * half)))
    decays = [(decay_ref[0:1, lo:lo + q], decay_ref[1:2, lo:lo + q]) for lo in (0, q)]

    def scan(j, carry):
        jb = jnp.where(j < ctx_chunks, ctx_chunks - 1 - j, n - 1 + ctx_chunks - j)
        out = []
        for row, (re, im), (a_re, a_im), (c_re, c_im) in zip((j, jb), parts, decays, carry):
            s_re, s_im = s_scr[pl.ds(row, 1), re], s_scr[pl.ds(row, 1), im]
            s_scr[pl.ds(row, 1), re] = c_re
            s_scr[pl.ds(row, 1), im] = c_im
            out.append((a_re * c_re - a_im * c_im + s_re, a_re * c_im + a_im * c_re + s_im))
        return tuple(out)

    zero = jnp.zeros((1, q), F32)
    lax.fori_loop(0, n, scan, ((zero, zero), (zero, zero)))
    h_ref[...] = s_scr[...].astype(h_ref.dtype)


def s5_state(u, w_state_s, decay_s, layer, *, ctx_chunks):
    k, n, cs = u.shape
    st = w_state_s.shape[-1]
    return pl.pallas_call(
        functools.partial(_s5_state_body, n=n, ctx_chunks=ctx_chunks),
        grid=(k,),
        in_specs=[
            pl.BlockSpec((None, n, cs), lambda i: (i, 0, 0)),
            pl.BlockSpec((None, None, cs, st), lambda i: (layer, i, 0, 0)),
            pl.BlockSpec((None, None, 2, st // 2), lambda i: (layer, i, 0, 0)),
        ],
        out_specs=pl.BlockSpec((None, n, st), lambda i: (i, 0, 0)),
        out_shape=jax.ShapeDtypeStruct((k, n, st), BF16),
        scratch_shapes=[pltpu.VMEM((n, st), F32)],
        compiler_params=_params("parallel"),
        name="s5_state",
    )(u, w_state_s, decay_s)


def _s5_readout_body(u_ref, h_ref, wu_ref, wh_ref, y_ref):
    y_ref[...] = (jnp.dot(u_ref[...], wu_ref[...], preferred_element_type=F32)
                  + jnp.dot(h_ref[...], wh_ref[...], preferred_element_type=F32))


def s5_readout(u, h, w_in_chunk, w_from_state, layer, *, tn):
    k, n, cs = u.shape
    st = h.shape[-1]
    return pl.pallas_call(
        _s5_readout_body,
        grid=(k, cs // tn),
        in_specs=[
            pl.BlockSpec((None, n, cs), lambda i, j: (i, 0, 0)),
            pl.BlockSpec((None, n, st), lambda i, j: (i, 0, 0)),
            pl.BlockSpec((None, None, cs, tn), lambda i, j: (layer, i, 0, j)),
            pl.BlockSpec((None, None, st, tn), lambda i, j: (layer, i, 0, j)),
        ],
        out_specs=pl.BlockSpec((None, n, tn), lambda i, j: (i, 0, j)),
        out_shape=jax.ShapeDtypeStruct((k, n, cs), F32),
        compiler_params=_params("parallel", "parallel"),
        name="s5_readout",
    )(u, h, w_in_chunk, w_from_state)


def _s5_glu_body(y_ref, u_ref, d_ref, w_ref, o_ref):
    z = jax.nn.gelu(y_ref[...] + d_ref[...] * u_ref[...].astype(F32))
    gate = _sigmoid(jnp.dot(z.astype(BF16), w_ref[...], preferred_element_type=F32))
    o_ref[...] = (z * gate).astype(o_ref.dtype)


def s5_glu(y, p, d, w_glu, layer, *, width, tr):
    t = p.shape[0]
    return pl.pallas_call(
        _s5_glu_body,
        grid=(t // tr,),
        in_specs=[
            pl.BlockSpec((tr, width), lambda i: (i, 0)),
            pl.BlockSpec((tr, width), lambda i: (i, 3)),
            pl.BlockSpec((None, 1, width), lambda i: (layer, 0, 0)),
            pl.BlockSpec((None, width, width), lambda i: (layer, 0, 0)),
        ],
        out_specs=pl.BlockSpec((tr, width), lambda i: (i, 0)),
        out_shape=jax.ShapeDtypeStruct((t, width), BF16),
        compiler_params=_params("parallel"),
        name="s5_glu",
    )(y, p, d.reshape(-1, 1, width), w_glu)


def _merge_body(gl_ref, yc_ref, ys_ref, yp_ref, ya_ref, gu_ref, gb_ref, wc_ref, ws_ref, wp_ref, wa_ref,
                o_ref):
    g_low = gl_ref[...]
    total = None
    for b, (y_ref, w_ref) in enumerate(((yc_ref, wc_ref), (ys_ref, ws_ref), (yp_ref, wp_ref),
                                        (ya_ref, wa_ref))):
        gate = _sigmoid(jnp.dot(g_low, gu_ref[b], preferred_element_type=F32) + gb_ref[b])
        term = gate * jnp.dot(y_ref[...], w_ref[...], preferred_element_type=F32)
        total = term if total is None else total + term
    o_ref[...] = total.astype(o_ref.dtype)


def merge(p, ys, gate_up, gate_bias, w_outs, layer, *, gate_col, tm, tn):
    t = p.shape[0]
    _, nb, rank, d = gate_up.shape
    row = lambda a: pl.BlockSpec((tm, a.shape[1]), lambda i, j: (i, 0))
    col = lambda a: pl.BlockSpec((None, a.shape[1], tn), lambda i, j: (layer, 0, j))
    return pl.pallas_call(
        _merge_body,
        grid=(t // tm, d // tn),
        in_specs=[pl.BlockSpec((tm, rank), lambda i, j: (i, gate_col))] + [row(y) for y in ys]
                 + [pl.BlockSpec((None, nb, rank, tn), lambda i, j: (layer, 0, 0, j)),
                    pl.BlockSpec((None, nb, 1, tn), lambda i, j: (layer, 0, 0, j))] + [col(w) for w in w_outs],
        out_specs=pl.BlockSpec((tm, tn), lambda i, j: (i, j)),
        out_shape=jax.ShapeDtypeStruct((t, d), BF16),
        compiler_params=_params("parallel", "parallel"),
        name="merge",
    )(p, *ys, gate_up, gate_bias.reshape(-1, nb, 1, d), *w_outs)


def _rope_tables(ctx_len, seq_len, dh):
    pairs = dh // 4
    rows = seq_len // GRID_W
    row = jnp.repeat(jnp.arange(rows), GRID_W).astype(F32)
    col = jnp.tile(jnp.arange(GRID_W), rows).astype(F32)
    inv = ROPE_THETA ** (-jnp.arange(pairs, dtype=F32) / pairs)
    ang = jnp.concatenate([row[:, None] * inv, col[:, None] * inv], axis=-1)
    cos = jnp.repeat(jnp.cos(ang), 2, axis=-1)
    sin = jnp.stack([-jnp.sin(ang), jnp.sin(ang)], axis=-1).reshape(seq_len, dh)
    cos = jnp.concatenate([jnp.ones((ctx_len, dh), F32), cos], axis=0)
    sin = jnp.concatenate([jnp.zeros((ctx_len, dh), F32), sin], axis=0)
    return cos, sin


def kernel(x, c, ctx, c_ctx, ada_down, ada_up, ada_bias, norm_gains, w_in, conv_w, s5_lambda_re, s5_lambda_im, s5_log_step, s5_b_re, s5_b_im, s5_c_re, s5_c_im, s5_d, s5_w_glu, pool_w, pool_scale, qk_norm, gate_up, gate_bias, w_out_conv, w_out_s5, w_out_pool, w_out_attn, w_o, ffn_w_gate, ffn_w_up, ffn_w_down):
    batch, seq_len, d = x.shape
    ctx_len = ctx.shape[1]
    depth = w_in.shape[0]
    assert batch == 1 and c.shape[0] == 1
    t = ctx_len + seq_len
    width = conv_w.shape[-1]
    assert s5_d.shape[-1] == width and pool_scale.shape[-1] == width
    dh = qk_norm.shape[-1]
    q_width = w_out_attn.shape[1]
    rank = gate_up.shape[2]
    in_width = w_in.shape[-1]
    kv_width = (in_width - 5 * width - q_width - rank) // 2
    heads, kv_heads = q_width // dh, kv_width // dh
    groups, s_dim = s5_b_re.shape[2], s5_b_re.shape[-1]
    slabs = width // LANES
    assert groups * s_dim == width and LANES % s_dim == 0
    hidden = ffn_w_gate.shape[-1]
    q_col, k_col, v_col = 5 * width, 5 * width + q_width, 5 * width + q_width + kv_width
    gate_col = (v_col + kv_width) // rank
    assert q_col % dh == 0 and (v_col + kv_width) % rank == 0 and width % dh == 0

    tr = _tile(ctx_len, 256, 8)
    tm = _tile(t, 1280, 256)
    tm_half = _tile(t, 640, 128)
    tk_att = _tile(t, 1280, 256)
    tq = _tile(ctx_len, 256, 128)
    ctx_tiles = ctx_len // tr

    w_in_b, w_o_b, w_gate_b, w_up_b, w_down_b = (w.astype(BF16) for w in (w_in, w_o, ffn_w_gate, ffn_w_up,
                                                                         ffn_w_down))
    w_outs_b = [w.astype(BF16) for w in (w_out_conv, w_out_s5, w_out_pool, w_out_attn)]
    gate_up_b, pool_w_b, w_glu_b = gate_up.astype(BF16), pool_w.astype(BF16), s5_w_glu.astype(BF16)
    w_state, w_s5top, w_s5bot, decay = s5_chunk_weights(s5_lambda_re, s5_lambda_im, s5_log_step,
                                               s5_b_re, s5_b_im, s5_c_re, s5_c_im)
    cos, sin = _rope_tables(ctx_len, seq_len, dh)
    qk_gains = qk_norm.reshape(depth, 2, 1, dh)

    cond = jnp.concatenate([c_ctx[None, :], c, jnp.zeros((6, d), F32)], axis=0)
    mods = ada_modulation(cond, ada_down, ada_up, ada_bias)
    mods = mods[:, :2].reshape(depth, 2, N_MOD, d)
    mods = jnp.concatenate([mods, jnp.zeros((depth, 2, 8 - N_MOD, d), F32)], axis=2)

    stream = jnp.concatenate([ctx[0], x[0]], axis=0)
    row_kw = dict(ctx_tiles=ctx_tiles, tr=tr)
    h = modulate(stream, mods, norm_gains, 0, rows=(0, 1, 0), **row_kw)
    n_chunks = t // S5_CHUNK
    for l in range(depth):
        p = matmul(h, w_in_b, l, tm=tm, tn=_tile(in_width, 512, 256), out_dtype=BF16)

        q_t = qk_prep(p, qk_gains, cos, sin, l, 0, col0=q_col // dh, heads=heads, tr=tk_att,
                      scale=dh ** -0.5 * math.log2(math.e), transpose=True)
        k_n = qk_prep(p, qk_gains, cos, sin, l, 1, col0=k_col // dh, heads=kv_heads, tr=tk_att,
                      scale=1.0, transpose=False)
        v_t = v_prep(p, col0=v_col // dh, heads=kv_heads, dh=dh, tk=tk_att)
        y_attn = flash_attention(q_t, k_n, v_t, tq=tq, ctx_len=ctx_len)

        y_conv, y_pool = conv_pool(p, conv_w, pool_w_b, pool_scale, l, width=width, tr=tr,
                                   ctx_len=ctx_len, seq_len=seq_len)

        u = p[:, 3 * width:4 * width].reshape(n_chunks, S5_CHUNK, slabs, LANES)
        u = jnp.transpose(u, (2, 0, 1, 3)).reshape(slabs, n_chunks, S5_CHUNK * LANES)
        h_s5 = s5_state(u, w_state, decay, l, ctx_chunks=ctx_len // S5_CHUNK)
        y_s5 = s5_readout(u, h_s5, w_s5top, w_s5bot, l, tn=_tile(S5_CHUNK * LANES, 1024, 256))
        y_s5 = jnp.transpose(y_s5.reshape(slabs, n_chunks, S5_CHUNK, LANES), (1, 2, 0, 3)).reshape(t, width)
        y_s5 = s5_glu(y_s5, p, s5_d, w_glu_b, l, width=width, tr=tm_half)

        merged = merge(p, [y_conv, y_s5, y_pool, y_attn], gate_up_b, gate_bias, w_outs_b, l,
                       gate_col=gate_col, tm=tm_half, tn=_tile(d, 512, 256))
        mix = matmul(merged, w_o_b, l, tm=tm, tn=_tile(d, 512, 256), out_dtype=BF16)
        stream, h = residual(stream, mix, mods, norm_gains, l, gate_row=2, post_row=1, nxt=(3, 4, 2), **row_kw)

        hid = ffn_up(h, w_gate_b, w_up_b, l, tm=tm, tn=_tile(hidden, 256, 128))
        ffn = matmul(hid, w_down_b, l, tm=tm, tn=_tile(d, 512, 256), nk=2)
        if l + 1 < depth:
            stream, h = residual(stream, ffn, mods, norm_gains, l, gate_row=5, post_row=3, nxt=(0, 1, 0),
                                 next_layer=l + 1, **row_kw)
        else:
            stream, _ = residual(stream, ffn, mods, norm_gains, l, gate_row=5, post_row=3,
                                 first_tile=ctx_tiles, **row_kw)
    return stream[None]
```

```python
import functools
import math

import jax
import jax.numpy as jnp
import numpy as np
from jax import lax
from jax.experimental import pallas as pl
from jax.experimental.pallas import tpu as pltpu

EPS = 1e-6
GRID_W = 64
ROPE_THETA = 10000.0
POOL_WINDOWS = (2, 4, 8, 16)
N_MOD = 6
S5_CHUNK = 16
HALO = 16
LANES = 128
ONES_ROWS = 16
NEG_BIG = -1e30
V7X_VMEM_LIMIT = 56 * 1024 * 1024
F32 = jnp.float32
BF16 = jnp.bfloat16
HIGHEST = lax.Precision.HIGHEST


def _params(*sem):
    return pltpu.CompilerParams(dimension_semantics=sem, vmem_limit_bytes=V7X_VMEM_LIMIT)


def _tile(n, target, mult):
    best = None
    for t in range(mult, min(n, target) + 1, mult):
        if n % t == 0:
            best = t
    assert best is not None, (n, target, mult)
    return best


def _rms(x, gain):
    return x * lax.rsqrt(jnp.mean(x * x, axis=-1, keepdims=True) + EPS) * gain


def _sigmoid(x):
    return 1.0 / (1.0 + jnp.exp(-x))


def _ada_body(cond_ref, down_ref, up_ref, bias_ref, o_ref):
    s = cond_ref[...]
    s = s * _sigmoid(s)
    low = jnp.dot(s, down_ref[...], precision=HIGHEST, preferred_element_type=F32)
    o_ref[...] = jnp.dot(low, up_ref[...], precision=HIGHEST, preferred_element_type=F32) + bias_ref[...]


def ada_modulation(cond, down, up, bias):
    depth, d, rank = down.shape
    width = up.shape[-1]
    tn = _tile(width, 4096, 128)
    return pl.pallas_call(
        _ada_body,
        grid=(depth, width // tn),
        in_specs=[
            pl.BlockSpec((8, d), lambda l, j: (0, 0)),
            pl.BlockSpec((None, d, rank), lambda l, j: (l, 0, 0)),
            pl.BlockSpec((None, rank, tn), lambda l, j: (l, 0, j)),
            pl.BlockSpec((None, 1, tn), lambda l, j: (l, 0, j)),
        ],
        out_specs=pl.BlockSpec((None, 8, tn), lambda l, j: (l, 0, j)),
        out_shape=jax.ShapeDtypeStruct((depth, 8, width), F32),
        compiler_params=_params("arbitrary", "arbitrary"),
        name="ada_modulation",
    )(cond, down, up, bias.reshape(depth, 1, width))


def _modulated(x, mod, gains, rows):
    shift_row, scale_row, gain_row = rows
    h = _rms(x, gains[gain_row:gain_row + 1])
    return h * (1.0 + mod[scale_row:scale_row + 1]) + mod[shift_row:shift_row + 1]


def _modulate_body(x_ref, mod_ref, gain_ref, h_ref, *, rows):
    h_ref[...] = _modulated(x_ref[...], mod_ref[...], gain_ref[...], rows).astype(h_ref.dtype)


def modulate(x, mods, gains, layer, *, ctx_tiles, tr, rows):
    t, d = x.shape
    return pl.pallas_call(
        functools.partial(_modulate_body, rows=rows),
        grid=(t // tr,),
        in_specs=[
            pl.BlockSpec((tr, d), lambda i: (i, 0)),
            pl.BlockSpec((None, None, 8, d), lambda i: (layer, jnp.where(i < ctx_tiles, 0, 1), 0, 0)),
            pl.BlockSpec((None,) + gains.shape[1:], lambda i: (layer, 0, 0)),
        ],
        out_specs=pl.BlockSpec((tr, d), lambda i: (i, 0)),
        out_shape=jax.ShapeDtypeStruct((t, d), BF16),
        compiler_params=_params("parallel"),
        name="modulate",
    )(x, mods, gains)


def _residual_body(x_ref, y_ref, mod_ref, gain_ref, modn_ref, gainn_ref, xo_ref, *h_ref,
                   gate_row, post_row, nxt):
    mod = mod_ref[...]
    xn = x_ref[...] + mod[gate_row:gate_row + 1] * _rms(y_ref[...].astype(F32), gain_ref[post_row:post_row + 1])
    xo_ref[...] = xn
    if nxt is not None:
        h_ref[0][...] = _modulated(xn, modn_ref[...], gainn_ref[...], nxt).astype(h_ref[0].dtype)


def residual(x, y, mods, gains, layer, *, ctx_tiles, tr, gate_row, post_row, nxt=None, next_layer=None,
             first_tile=0):
    t, d = x.shape
    n_tiles = t // tr - first_tile
    if next_layer is None:
        next_layer = layer
    who = lambda i: jnp.where(i + first_tile < ctx_tiles, 0, 1)
    row_in = pl.BlockSpec((tr, d), lambda i: (i + first_tile, 0))
    row_out = pl.BlockSpec((tr, d), lambda i: (i, 0))
    mod_spec = lambda l: pl.BlockSpec((None, None, 8, d), lambda i: (l, who(i), 0, 0))
    gain_spec = lambda l: pl.BlockSpec((None,) + gains.shape[1:], lambda i: (l, 0, 0))
    out_specs = [row_out]
    out_shape = [jax.ShapeDtypeStruct((n_tiles * tr, d), F32)]
    if nxt is not None:
        out_specs.append(row_out)
        out_shape.append(jax.ShapeDtypeStruct((n_tiles * tr, d), BF16))
    out = pl.pallas_call(
        functools.partial(_residual_body, gate_row=gate_row, post_row=post_row, nxt=nxt),
        grid=(n_tiles,),
        in_specs=[row_in, row_in, mod_spec(layer), gain_spec(layer), mod_spec(next_layer),
                  gain_spec(next_layer)],
        out_specs=out_specs,
        out_shape=out_shape,
        compiler_params=_params("parallel"),
        name="residual_norm",
    )(x, y, mods, gains, mods, gains)
    return out if nxt is not None else (out[0], None)


def _matmul_body(a_ref, b_ref, o_ref, *, nk):
    r = jnp.dot(a_ref[...], b_ref[...], preferred_element_type=F32)
    if nk == 1:
        o_ref[...] = r.astype(o_ref.dtype)
    else:
        k = pl.program_id(2)

        @pl.when(k == 0)
        def _():
            o_ref[...] = r

        @pl.when(k > 0)
        def _():
            o_ref[...] += r


def matmul(a, b, layer, *, tm, tn, nk=1, out_dtype=F32):
    m, k = a.shape
    n = b.shape[-1]
    assert k % nk == 0 and (nk == 1 or out_dtype == F32)
    tk = k // nk
    return pl.pallas_call(
        functools.partial(_matmul_body, nk=nk),
        grid=(m // tm, n // tn, nk),
        in_specs=[
            pl.BlockSpec((tm, tk), lambda i, j, kk: (i, kk)),
            pl.BlockSpec((None, tk, tn), lambda i, j, kk: (layer, kk, j)),
        ],
        out_specs=pl.BlockSpec((tm, tn), lambda i, j, kk: (i, j)),
        out_shape=jax.ShapeDtypeStruct((m, n), out_dtype),
        compiler_params=_params("parallel", "parallel", "arbitrary"),
        name="matmul",
    )(a, b)


def _ffn_up_body(a_ref, wg_ref, wu_ref, o_ref):
    a = a_ref[...]
    g = jnp.dot(a, wg_ref[...], preferred_element_type=F32)
    u = jnp.dot(a, wu_ref[...], preferred_element_type=F32)
    o_ref[...] = (g * _sigmoid(g) * u).astype(o_ref.dtype)


def ffn_up(a, w_gate, w_up, layer, *, tm, tn):
    m, k = a.shape
    n = w_gate.shape[-1]
    w_spec = pl.BlockSpec((None, k, tn), lambda i, j: (layer, 0, j))
    return pl.pallas_call(
        _ffn_up_body,
        grid=(m // tm, n // tn),
        in_specs=[pl.BlockSpec((tm, k), lambda i, j: (i, 0)), w_spec, w_spec],
        out_specs=pl.BlockSpec((tm, tn), lambda i, j: (i, j)),
        out_shape=jax.ShapeDtypeStruct((m, n), BF16),
        compiler_params=_params("parallel", "parallel"),
        name="ffn_up",
    )(a, w_gate, w_up)


def _qk_prep_body(x_ref, gain_ref, cos_ref, sin_ref, o_ref, *, scale, transpose):
    y = _rms(x_ref[...].astype(F32), gain_ref[...])
    lanes = y.shape[-1]
    even = lax.broadcasted_iota(jnp.int32, (1, lanes), 1) % 2 == 0
    partner = jnp.where(even, pltpu.roll(y, lanes - 1, 1), pltpu.roll(y, 1, 1))
    y = (y * cos_ref[...] + partner * sin_ref[...]) * scale
    if transpose:
        o_ref[...] = y.T.astype(o_ref.dtype)
    else:
        o_ref[...] = y.astype(o_ref.dtype)


def qk_prep(p, gains, cos, sin, layer, which, *, col0, heads, tr, scale, transpose):
    t = p.shape[0]
    dh = gains.shape[-1]
    if transpose:
        out_spec = pl.BlockSpec((None, dh, tr), lambda i, h: (h, 0, i))
        out_shape = jax.ShapeDtypeStruct((heads, dh, t), BF16)
    else:
        out_spec = pl.BlockSpec((None, tr, dh), lambda i, h: (h, i, 0))
        out_shape = jax.ShapeDtypeStruct((heads, t, dh), BF16)
    return pl.pallas_call(
        functools.partial(_qk_prep_body, scale=scale, transpose=transpose),
        grid=(t // tr, heads),
        in_specs=[
            pl.BlockSpec((tr, dh), lambda i, h: (i, col0 + h)),
            pl.BlockSpec((None, None, 1, dh), lambda i, h: (layer, which, 0, 0)),
            pl.BlockSpec((tr, dh), lambda i, h: (i, 0)),
            pl.BlockSpec((tr, dh), lambda i, h: (i, 0)),
        ],
        out_specs=out_spec,
        out_shape=out_shape,
        compiler_params=_params("parallel", "parallel"),
        name="qk_prep",
    )(p, gains, cos, sin)


def _v_prep_body(x_ref, o_ref):
    dh = x_ref.shape[1]
    o_ref[:dh] = x_ref[...].astype(F32).T.astype(o_ref.dtype)
    first = lax.broadcasted_iota(jnp.int32, (ONES_ROWS, o_ref.shape[1]), 0) == 0
    o_ref[dh:] = jnp.where(first, 1.0, 0.0).astype(o_ref.dtype)


def v_prep(p, *, col0, heads, dh, tk):
    t = p.shape[0]
    return pl.pallas_call(
        _v_prep_body,
        grid=(t // tk, heads),
        in_specs=[pl.BlockSpec((tk, dh), lambda i, h: (i, col0 + h))],
        out_specs=pl.BlockSpec((None, None, dh + ONES_ROWS, tk), lambda i, h: (h, i, 0, 0)),
        out_shape=jax.ShapeDtypeStruct((heads, t // tk, dh + ONES_ROWS, tk), BF16),
        compiler_params=_params("parallel", "parallel"),
        name="v_prep",
    )(p)


def _flash_body(q_ref, k_ref, v_ref, o_ref, s_a, s_b, m_a, m_b,
                *, tq, tk, group, halves, ctx_len, ctx_steps, n_chunks):
    dh = q_ref.shape[1]
    rows = v_ref.shape[1]
    streams = [(g, hf) for hf in range(halves) for g in range(group)]

    def scores(c, n, s_ref, m_ref, masked):
        g, hf = streams[n]
        k_c = k_ref[pl.ds(pl.multiple_of(c * tk, tk), tk), :]
        s = jnp.dot(k_c, q_ref[g, :, hf * tq:(hf + 1) * tq], preferred_element_type=F32)
        if masked:
            key = c * tk + lax.broadcasted_iota(jnp.int32, (tk, 1), 0)
            s = jnp.where(key < ctx_len, s, NEG_BIG)
        s_ref[n] = s
        m_ref[n] = jnp.max(s, axis=0, keepdims=True)

    def absorb(c, n, s_ref, m_ref, carry):
        m, acc = carry
        m_new = jnp.maximum(m, m_ref[n])
        p = jnp.exp2(s_ref[n] - m_new).astype(BF16)
        acc = jnp.exp2(m - m_new) * acc + jnp.dot(v_ref[c], p, preferred_element_type=F32)
        return m_new, acc

    def run(chunks, masked):
        ns = len(streams)
        carry = tuple((jnp.full((1, tq), NEG_BIG, F32), jnp.zeros((rows, tq), F32)) for _ in range(ns))
        for n in range(ns):
            scores(0, n, s_a, m_a, masked)

        def half(c_next, c_cur, nxt, cur, carry):
            out = []
            for n in range(ns):
                scores(c_next, n, *nxt, masked)
                out.append(absorb(c_cur, n, *cur, carry[n]))
            return tuple(out)

        def pair(c, carry):
            carry = half(c + 1, c, (s_b, m_b), (s_a, m_a), carry)
            return half(jnp.minimum(c + 2, chunks - 1), c + 1, (s_a, m_a), (s_b, m_b), carry)

        def quad(i, carry):
            return pair(4 * i + 2, pair(4 * i, carry))

        carry = lax.fori_loop(0, chunks // 4, quad, carry)
        if chunks % 4 >= 2:
            carry = pair(chunks // 4 * 4, carry)
        for n, (g, hf) in enumerate(streams):
            m, acc = carry[n]
            if chunks % 2 == 1:
                m, acc = absorb(chunks - 1, n, s_a, m_a, (m, acc))
            o = acc[:dh] / acc[dh:dh + 1]
            o_ref[hf * tq:(hf + 1) * tq, g * dh:(g + 1) * dh] = o.T.astype(o_ref.dtype)

    i = pl.program_id(1)

    @pl.when(i < ctx_steps)
    def _():
        run(-(-ctx_len // tk), True)

    @pl.when(i >= ctx_steps)
    def _():
        run(n_chunks, False)


def flash_attention(q_t, k, v_t, *, tq, ctx_len, halves=2):
    heads, dh, t = q_t.shape
    kv, n_chunks, rows, tk = v_t.shape
    group = heads // kv
    step = halves * tq
    pad = -ctx_len % step
    assert ctx_len % tq == 0 and (pad + t) % step == 0
    q_t = jnp.pad(q_t, ((0, 0), (0, 0), (pad, 0)))
    out = pl.pallas_call(
        functools.partial(_flash_body, tq=tq, tk=tk, group=group, halves=halves, ctx_len=ctx_len,
                          ctx_steps=(pad + ctx_len) // step, n_chunks=n_chunks),
        grid=(kv, (pad + t) // step),
        in_specs=[
            pl.BlockSpec((group, dh, step), lambda h, i: (h, 0, i)),
            pl.BlockSpec((None, t, dh), lambda h, i: (h, 0, 0)),
            pl.BlockSpec((None, n_chunks, rows, tk), lambda h, i: (h, 0, 0, 0)),
        ],
        out_specs=pl.BlockSpec((step, group * dh), lambda h, i: (i, h)),
        out_shape=jax.ShapeDtypeStruct((pad + t, heads * dh), BF16),
        scratch_shapes=[pltpu.VMEM((halves * group, tk, tq), F32), pltpu.VMEM((halves * group, tk, tq), F32),
                        pltpu.VMEM((halves * group, 1, tq), F32), pltpu.VMEM((halves * group, 1, tq), F32)],
        compiler_params=_params("parallel", "parallel"),
        name="flash_attention",
    )(q_t, k, v_t)
    return out[pad:]


def _convpool_body(ah_ref, ab_ref, ac_ref, pu_ref, ah_up, ac_up, pu_up, ah_dn, ac_dn, pu_dn,
                   cw_ref, pw_ref, ps_ref, yc_ref, yp_ref, xbuf, pbuf,
                   *, tr, ctx_tiles, n_tiles, ctx_len, seq_len):
    i = pl.program_id(0)
    has_up = jnp.logical_and(i != 0, i != ctx_tiles)
    has_dn = jnp.logical_and(i != ctx_tiles - 1, i != n_tiles - 1)

    f32 = lambda ref: ref[...].astype(F32)
    xbuf[0:HALO] = jnp.where(has_up, f32(ac_up) * f32(ah_up), 0.0)
    xbuf[HALO:HALO + tr] = f32(ac_ref) * f32(ah_ref)
    xbuf[HALO + tr:HALO + tr + HALO] = jnp.where(has_dn, f32(ac_dn) * f32(ah_dn), 0.0)
    conv = (cw_ref[0:1] * xbuf[HALO - 1:HALO - 1 + tr] + cw_ref[1:2] * xbuf[HALO:HALO + tr]
            + cw_ref[2:3] * xbuf[HALO + 1:HALO + 1 + tr])
    yc_ref[...] = (f32(ab_ref) * conv).astype(yc_ref.dtype)

    pbuf[0:HALO] = jnp.where(has_up, f32(pu_up), 0.0)
    pbuf[HALO:HALO + tr] = f32(pu_ref)
    pbuf[HALO + tr:HALO + tr + HALO] = jnp.where(has_dn, f32(pu_dn), 0.0)
    in_ctx = i < ctx_tiles
    pos = lax.broadcasted_iota(jnp.int32, (tr, 1), 0) + jnp.where(in_ctx, i, i - ctx_tiles) * tr
    length = jnp.where(in_ctx, ctx_len, seq_len)
    pg = pw_ref.shape[-1]
    for g, win in enumerate(POOL_WINDOWS):
        cols = slice(g * pg, (g + 1) * pg)
        total = None
        for s in range(-(win // 2), win - win // 2):
            piece = pbuf[HALO + s:HALO + s + tr, cols]
            total = piece if total is None else total + piece
        count = jnp.minimum(pos - win // 2 + win, length) - jnp.maximum(pos - win // 2, 0)
        pooled = total / count.astype(F32) - pbuf[HALO:HALO + tr, cols]
        y = jnp.dot(pooled.astype(BF16), pw_ref[g], preferred_element_type=F32) * ps_ref[:, cols]
        yp_ref[:, cols] = y.astype(yp_ref.dtype)


def conv_pool(p, conv_w, pool_w, pool_scale, layer, *, width, tr, ctx_len, seq_len):
    t = p.shape[0]
    n_tiles = t // tr
    ctx_tiles = ctx_len // tr
    assert ctx_len % tr == 0 and tr % HALO == 0 and max(POOL_WINDOWS) // 2 <= HALO
    hb = tr // HALO
    main = lambda c: pl.BlockSpec((tr, width), lambda i: (i, c))
    up = lambda c: pl.BlockSpec((HALO, width), lambda i: (jnp.maximum(i * hb - 1, 0), c))
    dn = lambda c: pl.BlockSpec((HALO, width), lambda i: (jnp.minimum((i + 1) * hb, t // HALO - 1), c))
    return pl.pallas_call(
        functools.partial(_convpool_body, tr=tr, ctx_tiles=ctx_tiles, n_tiles=n_tiles,
                          ctx_len=ctx_len, seq_len=seq_len),
        grid=(n_tiles,),
        in_specs=[main(0), main(1), main(2), main(4), up(0), up(2), up(4), dn(0), dn(2), dn(4),
                  pl.BlockSpec((None,) + conv_w.shape[1:], lambda i: (layer, 0, 0)),
                  pl.BlockSpec((None,) + pool_w.shape[1:], lambda i: (layer, 0, 0, 0)),
                  pl.BlockSpec((None, 1, width), lambda i: (layer, 0, 0))],
        out_specs=[pl.BlockSpec((tr, width), lambda i: (i, 0))] * 2,
        out_shape=[jax.ShapeDtypeStruct((t, width), BF16)] * 2,
        scratch_shapes=[pltpu.VMEM((tr + 2 * HALO, width), F32)] * 2,
        compiler_params=_params("parallel"),
        name="conv_pool",
    )(p, p, p, p, p, p, p, p, p, p, conv_w, pool_w, pool_scale.reshape(-1, 1, width))


def s5_chunk_weights(lam_re, lam_im, log_step, b_re, b_im, c_re, c_im):
    tc = S5_CHUNK
    lam = lax.complex(lam_re.astype(F32), lam_im.astype(F32))
    z = lam * jnp.exp(log_step.astype(F32))[..., None]
    lam_bar = jnp.exp(z)
    b_bar = ((lam_bar - 1.0) / lam)[..., None] * lax.complex(b_re.astype(F32), b_im.astype(F32))
    c = lax.complex(c_re.astype(F32), c_im.astype(F32))
    depth, _, groups, p_dim, s_dim = b_bar.shape
    powers = jnp.exp(z[..., None, :] * jnp.arange(tc + 1, dtype=F32)[:, None])

    cp = c[..., None, :, :] * powers[..., :tc, None, :]
    kern = (jnp.einsum('ldgtop,ldgpi->ldgtoi', jnp.real(cp), jnp.real(b_bar), precision=HIGHEST)
            - jnp.einsum('ldgtop,ldgpi->ldgtoi', jnp.imag(cp), jnp.imag(b_bar), precision=HIGHEST))
    k_f, k_b = kern[:, 0], kern[:, 1]
    by_lag = jnp.concatenate([k_b[:, :, :0:-1], k_f[:, :, :1] + k_b[:, :, :1], k_f[:, :, 1:],
                              jnp.zeros_like(k_f[:, :, :1])], axis=2)
    skew = jnp.tile(by_lag, (1, 1, tc, 1, 1))[:, :, :tc * (2 * tc - 1)]
    skew = skew.reshape(depth, groups, tc, 2 * tc - 1, s_dim, s_dim)[:, :, :, tc - 1:]
    toeplitz = jnp.transpose(skew, (0, 1, 2, 5, 3, 4)).reshape(depth, groups, tc * s_dim, tc * s_dim)

    f_f = c[:, 0][:, :, None] * powers[:, 0, :, 1:tc + 1, None, :]
    f_b = c[:, 1][:, :, None] * powers[:, 1, :, tc:0:-1, None, :]
    rows = lambda a: jnp.transpose(a, (0, 1, 4, 2, 3)).reshape(depth, groups, p_dim, tc * s_dim)
    w_out = jnp.concatenate([toeplitz, rows(jnp.real(f_f)), rows(jnp.real(f_b)),
                             rows(-jnp.imag(f_f)), rows(-jnp.imag(f_b))], axis=2)

    e_f = powers[:, 0, :, tc - 1::-1][:, :, :tc, :, None] * b_bar[:, 0][:, :, None]
    e_b = powers[:, 1, :, :tc, :, None] * b_bar[:, 1][:, :, None]
    cols = lambda a: jnp.transpose(a, (0, 1, 2, 4, 3)).reshape(depth, groups, tc * s_dim, p_dim)
    w_state = jnp.concatenate([cols(jnp.real(e_f)), cols(jnp.real(e_b)),
                               cols(jnp.imag(e_f)), cols(jnp.imag(e_b))], axis=3)

    last = powers[..., tc, :]
    decay = jnp.stack([jnp.concatenate([jnp.real(last[:, 0]), jnp.real(last[:, 1])], axis=-1),
                       jnp.concatenate([jnp.imag(last[:, 0]), jnp.imag(last[:, 1])], axis=-1)], axis=2)
    return _slab_operators(w_state, w_out, decay, tc, s_dim)


def _slab_operators(w_state, w_out, decay, tc, s_dim):
    depth, groups, cs, st = w_state.shape
    gl = LANES // s_dim
    k = groups // gl
    p_dim = st // 4
    w_state, w_out = lax.optimization_barrier((w_state.astype(BF16), w_out.astype(BF16)))

    def stacked_rows(a, piece):
        rows, cols = a.shape[2:]
        a = a.reshape(depth, k, gl, rows // piece, piece, cols)
        return jnp.transpose(a, (0, 1, 3, 2, 4, 5)).reshape(depth, k, gl * rows, cols).astype(BF16)

    def spread(cols, piece):
        c = np.arange(gl * cols)
        src = c // (gl * piece) * piece + c % piece
        return jnp.asarray(np.arange(cols)[:, None] == src[None, :], dtype=BF16)

    ws = expand_block_diagonal(stacked_rows(w_state, s_dim), spread(st, p_dim), s_dim, p_dim, gl)
    top = expand_block_diagonal(stacked_rows(w_out[:, :, :cs], s_dim), spread(cs, s_dim), s_dim, s_dim, gl)
    bot = expand_block_diagonal(stacked_rows(w_out[:, :, cs:], p_dim), spread(cs, s_dim), p_dim, s_dim, gl)
    decay_s = jnp.transpose(decay.reshape(depth, k, gl, 2, 2, p_dim), (0, 1, 3, 4, 2, 5))
    return ws, top, bot, decay_s.reshape(depth, k, 2, 2 * gl * p_dim)


def _expand_body(a_ref, c_ref, o_ref, *, row_piece, col_piece, gl):
    r = jnp.dot(a_ref[...], c_ref[...], preferred_element_type=F32)
    row0 = pl.program_id(2) * a_ref.shape[0]
    row_group = (row0 + lax.broadcasted_iota(jnp.int32, (a_ref.shape[0], 1), 0)) // row_piece % gl
    col_group = lax.broadcasted_iota(jnp.int32, (1, c_ref.shape[1]), 1) // col_piece % gl
    o_ref[...] = jnp.where(row_group == col_group, r, 0.0).astype(o_ref.dtype)


def expand_block_diagonal(a, spread, row_piece, col_piece, gl):
    depth, k, rows, cols = a.shape
    tr = _tile(rows, 512, 16)
    return pl.pallas_call(
        functools.partial(_expand_body, row_piece=row_piece, col_piece=col_piece, gl=gl),
        grid=(depth, k, rows // tr),
        in_specs=[pl.BlockSpec((None, None, tr, cols), lambda l, i, r: (l, i, r, 0)),
                  pl.BlockSpec(spread.shape, lambda l, i, r: (0, 0))],
        out_specs=pl.BlockSpec((None, None, tr, gl * cols), lambda l, i, r: (l, i, r, 0)),
        out_shape=jax.ShapeDtypeStruct((depth, k, rows, gl * cols), BF16),
        compiler_params=_params("parallel", "parallel", "parallel"),
        name="expand_block_diagonal",
    )(a, spread)


def _s5_state_body(u_ref, w_ref, decay_ref, h_ref, s_scr, *, n, ctx_chunks):
    half = decay_ref.shape[-1]
    q = half // 2
    rows = n // 2 if n % 16 == 0 else n
    for r in range(0, n, rows):
        s_scr[r:r + rows] = jnp.dot(u_ref[r:r + rows], w_ref[...], preferred_element_type=F32)
    parts = ((slice(0, q), slice(half, half + q)), (slice(q, half), slice(half + q, 2 * half)))
    decays = [(decay_ref[0:1, lo:lo + q], decay_ref[1:2, lo:lo + q]) for lo in (0, q)]

    def scan(j, carry):
        jb = jnp.where(j < ctx_chunks, ctx_chunks - 1 - j, n - 1 + ctx_chunks - j)
        out = []
        for row, (re, im), (a_re, a_im), (c_re, c_im) in zip((j, jb), parts, decays, carry):
            s_re, s_im = s_scr[pl.ds(row, 1), re], s_scr[pl.ds(row, 1), im]
            s_scr[pl.ds(row, 1), re] = c_re
            s_scr[pl.ds(row, 1), im] = c_im
            out.append((a_re * c_re - a_im * c_im + s_re, a_re * c_im + a_im * c_re + s_im))
        return tuple(out)

    zero = jnp.zeros((1, q), F32)
    lax.fori_loop(0, n, scan, ((zero, zero), (zero, zero)))
    h_ref[...] = s_scr[...].astype(h_ref.dtype)


def s5_state(u, w_state_s, decay_s, layer, *, ctx_chunks):
    k, n, cs = u.shape
    st = w_state_s.shape[-1]
    return pl.pallas_call(
        functools.partial(_s5_state_body, n=n, ctx_chunks=ctx_chunks),
        grid=(k,),
        in_specs=[
            pl.BlockSpec((None, n, cs), lambda i: (i, 0, 0)),
            pl.BlockSpec((None, None, cs, st), lambda i: (layer, i, 0, 0)),
            pl.BlockSpec((None, None, 2, st // 2), lambda i: (layer, i, 0, 0)),
        ],
        out_specs=pl.BlockSpec((None, n, st), lambda i: (i, 0, 0)),
        out_shape=jax.ShapeDtypeStruct((k, n, st), BF16),
        scratch_shapes=[pltpu.VMEM((n, st), F32)],
        compiler_params=_params("parallel"),
        name="s5_state",
    )(u, w_state_s, decay_s)


def _s5_readout_body(u_ref, h_ref, wu_ref, wh_ref, y_ref):
    y_ref[...] = (jnp.dot(u_ref[...], wu_ref[...], preferred_element_type=F32)
                  + jnp.dot(h_ref[...], wh_ref[...], preferred_element_type=F32))


def s5_readout(u, h, w_in_chunk, w_from_state, layer, *, tn):
    k, n, cs = u.shape
    st = h.shape[-1]
    return pl.pallas_call(
        _s5_readout_body,
        grid=(k, cs // tn),
        in_specs=[
            pl.BlockSpec((None, n, cs), lambda i, j: (i, 0, 0)),
            pl.BlockSpec((None, n, st), lambda i, j: (i, 0, 0)),
            pl.BlockSpec((None, None, cs, tn), lambda i, j: (layer, i, 0, j)),
            pl.BlockSpec((None, None, st, tn), lambda i, j: (layer, i, 0, j)),
        ],
        out_specs=pl.BlockSpec((None, n, tn), lambda i, j: (i, 0, j)),
        out_shape=jax.ShapeDtypeStruct((k, n, cs), F32),
        compiler_params=_params("parallel", "parallel"),
        name="s5_readout",
    )(u, h, w_in_chunk, w_from_state)


def _s5_glu_body(y_ref, u_ref, d_ref, w_ref, o_ref):
    z = jax.nn.gelu(y_ref[...] + d_ref[...] * u_ref[...].astype(F32))
    gate = _sigmoid(jnp.dot(z.astype(BF16), w_ref[...], preferred_element_type=F32))
    o_ref[...] = (z * gate).astype(o_ref.dtype)


def s5_glu(y, p, d, w_glu, layer, *, width, tr):
    t = p.shape[0]
    return pl.pallas_call(
        _s5_glu_body,
        grid=(t // tr,),
        in_specs=[
            pl.BlockSpec((tr, width), lambda i: (i, 0)),
            pl.BlockSpec((tr, width), lambda i: (i, 3)),
            pl.BlockSpec((None, 1, width), lambda i: (layer, 0, 0)),
            pl.BlockSpec((None, width, width), lambda i: (layer, 0, 0)),
        ],
        out_specs=pl.BlockSpec((tr, width), lambda i: (i, 0)),
        out_shape=jax.ShapeDtypeStruct((t, width), BF16),
        compiler_params=_params("parallel"),
        name="s5_glu",
    )(y, p, d.reshape(-1, 1, width), w_glu)


def _merge_body(gl_ref, yc_ref, ys_ref, yp_ref, ya_ref, gu_ref, gb_ref, wc_ref, ws_ref, wp_ref, wa_ref,
                o_ref):
    g_low = gl_ref[...]
    total = None
    for b, (y_ref, w_ref) in enumerate(((yc_ref, wc_ref), (ys_ref, ws_ref), (yp_ref, wp_ref),
                                        (ya_ref, wa_ref))):
        gate = _sigmoid(jnp.dot(g_low, gu_ref[b], preferred_element_type=F32) + gb_ref[b])
        term = gate * jnp.dot(y_ref[...], w_ref[...], preferred_element_type=F32)
        total = term if total is None else total + term
    o_ref[...] = total.astype(o_ref.dtype)


def merge(p, ys, gate_up, gate_bias, w_outs, layer, *, gate_col, tm, tn):
    t = p.shape[0]
    _, nb, rank, d = gate_up.shape
    row = lambda a: pl.BlockSpec((tm, a.shape[1]), lambda i, j: (i, 0))
    col = lambda a: pl.BlockSpec((None, a.shape[1], tn), lambda i, j: (layer, 0, j))
    return pl.pallas_call(
        _merge_body,
        grid=(t // tm, d // tn),
        in_specs=[pl.BlockSpec((tm, rank), lambda i, j: (i, gate_col))] + [row(y) for y in ys]
                 + [pl.BlockSpec((None, nb, rank, tn), lambda i, j: (layer, 0, 0, j)),
                    pl.BlockSpec((None, nb, 1, tn), lambda i, j: (layer, 0, 0, j))] + [col(w) for w in w_outs],
        out_specs=pl.BlockSpec((tm, tn), lambda i, j: (i, j)),
        out_shape=jax.ShapeDtypeStruct((t, d), BF16),
        compiler_params=_params("parallel", "parallel"),
        name="merge",
    )(p, *ys, gate_up, gate_bias.reshape(-1, nb, 1, d), *w_outs)


def _rope_tables(ctx_len, seq_len, dh):
    pairs = dh // 4
    rows = seq_len // GRID_W
    row = jnp.repeat(jnp.arange(rows), GRID_W).astype(F32)
    col = jnp.tile(jnp.arange(GRID_W), rows).astype(F32)
    inv = ROPE_THETA ** (-jnp.arange(pairs, dtype=F32) / pairs)
    ang = jnp.concatenate([row[:, None] * inv, col[:, None] * inv], axis=-1)
    cos = jnp.repeat(jnp.cos(ang), 2, axis=-1)
    sin = jnp.stack([-jnp.sin(ang), jnp.sin(ang)], axis=-1).reshape(seq_len, dh)
    cos = jnp.concatenate([jnp.ones((ctx_len, dh), F32), cos], axis=0)
    sin = jnp.concatenate([jnp.zeros((ctx_len, dh), F32), sin], axis=0)
    return cos, sin


def kernel(x, c, ctx, c_ctx, ada_down, ada_up, ada_bias, norm_gains, w_in, conv_w, s5_lambda_re, s5_lambda_im, s5_log_step, s5_b_re, s5_b_im, s5_c_re, s5_c_im, s5_d, s5_w_glu, pool_w, pool_scale, qk_norm, gate_up, gate_bias, w_out_conv, w_out_s5, w_out_pool, w_out_attn, w_o, ffn_w_gate, ffn_w_up, ffn_w_down):
    batch, seq_len, d = x.shape
    ctx_len = ctx.shape[1]
    depth = w_in.shape[0]
    assert batch == 1 and c.shape[0] == 1
    t = ctx_len + seq_len
    width = conv_w.shape[-1]
    assert s5_d.shape[-1] == width and pool_scale.shape[-1] == width
    dh = qk_norm.shape[-1]
    q_width = w_out_attn.shape[1]
    rank = gate_up.shape[2]
    in_width = w_in.shape[-1]
    kv_width = (in_width - 5 * width - q_width - rank) // 2
    heads, kv_heads = q_width // dh, kv_width // dh
    groups, s_dim = s5_b_re.shape[2], s5_b_re.shape[-1]
    slabs = width // LANES
    assert groups * s_dim == width and LANES % s_dim == 0
    hidden = ffn_w_gate.shape[-1]
    q_col, k_col, v_col = 5 * width, 5 * width + q_width, 5 * width + q_width + kv_width
    gate_col = (v_col + kv_width) // rank
    assert q_col % dh == 0 and (v_col + kv_width) % rank == 0 and width % dh == 0

    tr = _tile(ctx_len, 256, 8)
    tm = _tile(t, 1280, 256)
    tm_half = _tile(t, 640, 128)
    tk_att = _tile(t, 1280, 256)
    tq = _tile(ctx_len, 256, 128)
    ctx_tiles = ctx_len // tr

    w_in_b, w_o_b, w_gate_b, w_up_b, w_down_b = (w.astype(BF16) for w in (w_in, w_o, ffn_w_gate, ffn_w_up,
                                                                         ffn_w_down))
    w_outs_b = [w.astype(BF16) for w in (w_out_conv, w_out_s5, w_out_pool, w_out_attn)]
    gate_up_b, pool_w_b, w_glu_b = gate_up.astype(BF16), pool_w.astype(BF16), s5_w_glu.astype(BF16)
    w_state, w_s5top, w_s5bot, decay = s5_chunk_weights(s5_lambda_re, s5_lambda_im, s5_log_step,
                                               s5_b_re, s5_b_im, s5_c_re, s5_c_im)
    cos, sin = _rope_tables(ctx_len, seq_len, dh)
    qk_gains = qk_norm.reshape(depth, 2, 1, dh)

    cond = jnp.concatenate([c_ctx[None, :], c, jnp.zeros((6, d), F32)], axis=0)
    mods = ada_modulation(cond, ada_down, ada_up, ada_bias)
    mods = mods[:, :2].reshape(depth, 2, N_MOD, d)
    mods = jnp.concatenate([mods, jnp.zeros((depth, 2, 8 - N_MOD, d), F32)], axis=2)

    stream = jnp.concatenate([ctx[0], x[0]], axis=0)
    row_kw = dict(ctx_tiles=ctx_tiles, tr=tr)
    h = modulate(stream, mods, norm_gains, 0, rows=(0, 1, 0), **row_kw)
    n_chunks = t // S5_CHUNK
    for l in range(depth):
        p = matmul(h, w_in_b, l, tm=tm, tn=_tile(in_width, 512, 256), out_dtype=BF16)

        q_t = qk_prep(p, qk_gains, cos, sin, l, 0, col0=q_col // dh, heads=heads, tr=tk_att,
                      scale=dh ** -0.5 * math.log2(math.e), transpose=True)
        k_n = qk_prep(p, qk_gains, cos, sin, l, 1, col0=k_col // dh, heads=kv_heads, tr=tk_att,
                      scale=1.0, transpose=False)
        v_t = v_prep(p, col0=v_col // dh, heads=kv_heads, dh=dh, tk=tk_att)
        y_attn = flash_attention(q_t, k_n, v_t, tq=tq, ctx_len=ctx_len)

        y_conv, y_pool = conv_pool(p, conv_w, pool_w_b, pool_scale, l, width=width, tr=tr,
                                   ctx_len=ctx_len, seq_len=seq_len)

        u = p[:, 3 * width:4 * width].reshape(n_chunks, S5_CHUNK, slabs, LANES)
        u = jnp.transpose(u, (2, 0, 1, 3)).reshape(slabs, n_chunks, S5_CHUNK * LANES)
        h_s5 = s5_state(u, w_state, decay, l, ctx_chunks=ctx_len // S5_CHUNK)
        y_s5 = s5_readout(u, h_s5, w_s5top, w_s5bot, l, tn=_tile(S5_CHUNK * LANES, 1024, 256))
        y_s5 = jnp.transpose(y_s5.reshape(slabs, n_chunks, S5_CHUNK, LANES), (1, 2, 0, 3)).reshape(t, width)
        y_s5 = s5_glu(y_s5, p, s5_d, w_glu_b, l, width=width, tr=tm_half)

        merged = merge(p, [y_conv, y_s5, y_pool, y_attn], gate_up_b, gate_bias, w_outs_b, l,
                       gate_col=gate_col, tm=tm_half, tn=_tile(d, 512, 256))
        mix = matmul(merged, w_o_b, l, tm=tm, tn=_tile(d, 512, 256), out_dtype=BF16)
        stream, h = residual(stream, mix, mods, norm_gains, l, gate_row=2, post_row=1, nxt=(3, 4, 2), **row_kw)

        hid = ffn_up(h, w_gate_b, w_up_b, l, tm=_tile(t, 1664, 128), tn=_tile(hidden, 256, 128))
        ffn = matmul(hid, w_down_b, l, tm=tm, tn=_tile(d, 512, 256), nk=2)
        if l + 1 < depth:
            stream, h = residual(stream, ffn, mods, norm_gains, l, gate_row=5, post_row=3, nxt=(0, 1, 0),
                                 next_layer=l + 1, **row_kw)
        else:
            stream, _ = residual(stream, ffn, mods, norm_gains, l, gate_row=5, post_row=3,
                                 first_tile=ctx_tiles, **row_kw)
    return stream[None]
```

```python
import functools
import math

import jax
import jax.numpy as jnp
import numpy as np
from jax import lax
from jax.experimental import pallas as pl
from jax.experimental.pallas import tpu as pltpu

EPS = 1e-6
GRID_W = 64
ROPE_THETA = 10000.0
POOL_WINDOWS = (2, 4, 8, 16)
N_MOD = 6
S5_CHUNK = 16
HALO = 16
LANES = 128
ONES_ROWS = 16
NEG_BIG = -1e30
V7X_VMEM_LIMIT = 56 * 1024 * 1024
F32 = jnp.float32
BF16 = jnp.bfloat16
HIGHEST = lax.Precision.HIGHEST


def _params(*sem):
    return pltpu.CompilerParams(dimension_semantics=sem, vmem_limit_bytes=V7X_VMEM_LIMIT)


def _tile(n, target, mult):
    best = None
    for t in range(mult, min(n, target) + 1, mult):
        if n % t == 0:
            best = t
    assert best is not None, (n, target, mult)
    return best


def _rms(x, gain):
    return x * lax.rsqrt(jnp.mean(x * x, axis=-1, keepdims=True) + EPS) * gain


def _sigmoid(x):
    return 1.0 / (1.0 + jnp.exp(-x))


def _ada_body(cond_ref, down_ref, up_ref, bias_ref, o_ref):
    s = cond_ref[...]
    s = s * _sigmoid(s)
    low = jnp.dot(s, down_ref[...], precision=HIGHEST, preferred_element_type=F32)
    o_ref[...] = jnp.dot(low, up_ref[...], precision=HIGHEST, preferred_element_type=F32) + bias_ref[...]


def ada_modulation(cond, down, up, bias):
    depth, d, rank = down.shape
    width = up.shape[-1]
    tn = _tile(width, 4096, 128)
    return pl.pallas_call(
        _ada_body,
        grid=(depth, width // tn),
        in_specs=[
            pl.BlockSpec((8, d), lambda l, j: (0, 0)),
            pl.BlockSpec((None, d, rank), lambda l, j: (l, 0, 0)),
            pl.BlockSpec((None, rank, tn), lambda l, j: (l, 0, j)),
            pl.BlockSpec((None, 1, tn), lambda l, j: (l, 0, j)),
        ],
        out_specs=pl.BlockSpec((None, 8, tn), lambda l, j: (l, 0, j)),
        out_shape=jax.ShapeDtypeStruct((depth, 8, width), F32),
        compiler_params=_params("arbitrary", "arbitrary"),
        name="ada_modulation",
    )(cond, down, up, bias.reshape(depth, 1, width))


def _modulated(x, mod, gains, rows):
    shift_row, scale_row, gain_row = rows
    h = _rms(x, gains[gain_row:gain_row + 1])
    return h * (1.0 + mod[scale_row:scale_row + 1]) + mod[shift_row:shift_row + 1]


def _modulate_body(x_ref, mod_ref, gain_ref, h_ref, *, rows):
    h_ref[...] = _modulated(x_ref[...], mod_ref[...], gain_ref[...], rows).astype(h_ref.dtype)


def modulate(x, mods, gains, layer, *, ctx_tiles, tr, rows):
    t, d = x.shape
    return pl.pallas_call(
        functools.partial(_modulate_body, rows=rows),
        grid=(t // tr,),
        in_specs=[
            pl.BlockSpec((tr, d), lambda i: (i, 0)),
            pl.BlockSpec((None, None, 8, d), lambda i: (layer, jnp.where(i < ctx_tiles, 0, 1), 0, 0)),
            pl.BlockSpec((None,) + gains.shape[1:], lambda i: (layer, 0, 0)),
        ],
        out_specs=pl.BlockSpec((tr, d), lambda i: (i, 0)),
        out_shape=jax.ShapeDtypeStruct((t, d), BF16),
        compiler_params=_params("parallel"),
        name="modulate",
    )(x, mods, gains)


def _residual_body(x_ref, y_ref, mod_ref, gain_ref, modn_ref, gainn_ref, xo_ref, *h_ref,
                   gate_row, post_row, nxt):
    mod = mod_ref[...]
    xn = x_ref[...] + mod[gate_row:gate_row + 1] * _rms(y_ref[...].astype(F32), gain_ref[post_row:post_row + 1])
    xo_ref[...] = xn
    if nxt is not None:
        h_ref[0][...] = _modulated(xn, modn_ref[...], gainn_ref[...], nxt).astype(h_ref[0].dtype)


def residual(x, y, mods, gains, layer, *, ctx_tiles, tr, gate_row, post_row, nxt=None, next_layer=None,
             first_tile=0):
    t, d = x.shape
    n_tiles = t // tr - first_tile
    if next_layer is None:
        next_layer = layer
    who = lambda i: jnp.where(i + first_tile < ctx_tiles, 0, 1)
    row_in = pl.BlockSpec((tr, d), lambda i: (i + first_tile, 0))
    row_out = pl.BlockSpec((tr, d), lambda i: (i, 0))
    mod_spec = lambda l: pl.BlockSpec((None, None, 8, d), lambda i: (l, who(i), 0, 0))
    gain_spec = lambda l: pl.BlockSpec((None,) + gains.shape[1:], lambda i: (l, 0, 0))
    out_specs = [row_out]
    out_shape = [jax.ShapeDtypeStruct((n_tiles * tr, d), F32)]
    if nxt is not None:
        out_specs.append(row_out)
        out_shape.append(jax.ShapeDtypeStruct((n_tiles * tr, d), BF16))
    out = pl.pallas_call(
        functools.partial(_residual_body, gate_row=gate_row, post_row=post_row, nxt=nxt),
        grid=(n_tiles,),
        in_specs=[row_in, row_in, mod_spec(layer), gain_spec(layer), mod_spec(next_layer),
                  gain_spec(next_layer)],
        out_specs=out_specs,
        out_shape=out_shape,
        compiler_params=_params("parallel"),
        name="residual_norm",
    )(x, y, mods, gains, mods, gains)
    return out if nxt is not None else (out[0], None)


def _matmul_body(a_ref, b_ref, o_ref, *, nk):
    r = jnp.dot(a_ref[...], b_ref[...], preferred_element_type=F32)
    if nk == 1:
        o_ref[...] = r.astype(o_ref.dtype)
    else:
        k = pl.program_id(2)

        @pl.when(k == 0)
        def _():
            o_ref[...] = r

        @pl.when(k > 0)
        def _():
            o_ref[...] += r


def matmul(a, b, layer, *, tm, tn, nk=1, out_dtype=F32):
    m, k = a.shape
    n = b.shape[-1]
    assert k % nk == 0 and (nk == 1 or out_dtype == F32)
    tk = k // nk
    return pl.pallas_call(
        functools.partial(_matmul_body, nk=nk),
        grid=(m // tm, n // tn, nk),
        in_specs=[
            pl.BlockSpec((tm, tk), lambda i, j, kk: (i, kk)),
            pl.BlockSpec((None, tk, tn), lambda i, j, kk: (layer, kk, j)),
        ],
        out_specs=pl.BlockSpec((tm, tn), lambda i, j, kk: (i, j)),
        out_shape=jax.ShapeDtypeStruct((m, n), out_dtype),
        compiler_params=_params("parallel", "parallel", "arbitrary"),
        name="matmul",
    )(a, b)


def _ffn_up_body(a_ref, wg_ref, wu_ref, o_ref):
    a = a_ref[...]
    g = jnp.dot(a, wg_ref[...], preferred_element_type=F32)
    u = jnp.dot(a, wu_ref[...], preferred_element_type=F32)
    o_ref[...] = (g * _sigmoid(g) * u).astype(o_ref.dtype)


def ffn_up(a, w_gate, w_up, layer, *, tm, tn):
    m, k = a.shape
    n = w_gate.shape[-1]
    w_spec = pl.BlockSpec((None, k, tn), lambda i, j: (layer, 0, j))
    return pl.pallas_call(
        _ffn_up_body,
        grid=(m // tm, n // tn),
        in_specs=[pl.BlockSpec((tm, k), lambda i, j: (i, 0)), w_spec, w_spec],
        out_specs=pl.BlockSpec((tm, tn), lambda i, j: (i, j)),
        out_shape=jax.ShapeDtypeStruct((m, n), BF16),
        compiler_params=_params("parallel", "parallel"),
        name="ffn_up",
    )(a, w_gate, w_up)


def _qk_prep_body(x_ref, gain_ref, cos_ref, sin_ref, o_ref, *, scale, transpose):
    y = _rms(x_ref[...].astype(F32), gain_ref[...])
    lanes = y.shape[-1]
    even = lax.broadcasted_iota(jnp.int32, (1, lanes), 1) % 2 == 0
    partner = jnp.where(even, pltpu.roll(y, lanes - 1, 1), pltpu.roll(y, 1, 1))
    y = (y * cos_ref[...] + partner * sin_ref[...]) * scale
    if transpose:
        o_ref[...] = y.T.astype(o_ref.dtype)
    else:
        o_ref[...] = y.astype(o_ref.dtype)


def qk_prep(p, gains, cos, sin, layer, which, *, col0, heads, tr, scale, transpose):
    t = p.shape[0]
    dh = gains.shape[-1]
    if transpose:
        out_spec = pl.BlockSpec((None, dh, tr), lambda i, h: (h, 0, i))
        out_shape = jax.ShapeDtypeStruct((heads, dh, t), BF16)
    else:
        out_spec = pl.BlockSpec((None, tr, dh), lambda i, h: (h, i, 0))
        out_shape = jax.ShapeDtypeStruct((heads, t, dh), BF16)
    return pl.pallas_call(
        functools.partial(_qk_prep_body, scale=scale, transpose=transpose),
        grid=(t // tr, heads),
        in_specs=[
            pl.BlockSpec((tr, dh), lambda i, h: (i, col0 + h)),
            pl.BlockSpec((None, None, 1, dh), lambda i, h: (layer, which, 0, 0)),
            pl.BlockSpec((tr, dh), lambda i, h: (i, 0)),
            pl.BlockSpec((tr, dh), lambda i, h: (i, 0)),
        ],
        out_specs=out_spec,
        out_shape=out_shape,
        compiler_params=_params("parallel", "parallel"),
        name="qk_prep",
    )(p, gains, cos, sin)


def _v_prep_body(x_ref, o_ref):
    dh = x_ref.shape[1]
    o_ref[:dh] = x_ref[...].astype(F32).T.astype(o_ref.dtype)
    first = lax.broadcasted_iota(jnp.int32, (ONES_ROWS, o_ref.shape[1]), 0) == 0
    o_ref[dh:] = jnp.where(first, 1.0, 0.0).astype(o_ref.dtype)


def v_prep(p, *, col0, heads, dh, tk):
    t = p.shape[0]
    return pl.pallas_call(
        _v_prep_body,
        grid=(t // tk, heads),
        in_specs=[pl.BlockSpec((tk, dh), lambda i, h: (i, col0 + h))],
        out_specs=pl.BlockSpec((None, None, dh + ONES_ROWS, tk), lambda i, h: (h, i, 0, 0)),
        out_shape=jax.ShapeDtypeStruct((heads, t // tk, dh + ONES_ROWS, tk), BF16),
        compiler_params=_params("parallel", "parallel"),
        name="v_prep",
    )(p)


def _flash_body(q_ref, k_ref, v_ref, o_ref, s_a, s_b, m_a, m_b,
                *, tq, tk, group, halves, ctx_len, ctx_steps, n_chunks):
    dh = q_ref.shape[1]
    rows = v_ref.shape[1]
    streams = [(g, hf) for hf in range(halves) for g in range(group)]

    def scores(c, n, s_ref, m_ref, masked):
        g, hf = streams[n]
        k_c = k_ref[pl.ds(pl.multiple_of(c * tk, tk), tk), :]
        s = jnp.dot(k_c, q_ref[g, :, hf * tq:(hf + 1) * tq], preferred_element_type=F32)
        if masked:
            key = c * tk + lax.broadcasted_iota(jnp.int32, (tk, 1), 0)
            s = jnp.where(key < ctx_len, s, NEG_BIG)
        s_ref[n] = s
        m_ref[n] = jnp.max(s, axis=0, keepdims=True)

    def absorb(c, n, s_ref, m_ref, carry):
        m, acc = carry
        m_new = jnp.maximum(m, m_ref[n])
        p = jnp.exp2(s_ref[n] - m_new).astype(BF16)
        acc = jnp.exp2(m - m_new) * acc + jnp.dot(v_ref[c], p, preferred_element_type=F32)
        return m_new, acc

    def run(chunks, masked):
        ns = len(streams)
        carry = tuple((jnp.full((1, tq), NEG_BIG, F32), jnp.zeros((rows, tq), F32)) for _ in range(ns))
        for n in range(ns):
            scores(0, n, s_a, m_a, masked)

        def half(c_next, c_cur, nxt, cur, carry):
            out = []
            for n in range(ns):
                scores(c_next, n, *nxt, masked)
                out.append(absorb(c_cur, n, *cur, carry[n]))
            return tuple(out)

        def pair(c, carry):
            carry = half(c + 1, c, (s_b, m_b), (s_a, m_a), carry)
            return half(jnp.minimum(c + 2, chunks - 1), c + 1, (s_a, m_a), (s_b, m_b), carry)

        def quad(i, carry):
            return pair(4 * i + 2, pair(4 * i, carry))

        carry = lax.fori_loop(0, chunks // 4, quad, carry)
        if chunks % 4 >= 2:
            carry = pair(chunks // 4 * 4, carry)
        for n, (g, hf) in enumerate(streams):
            m, acc = carry[n]
            if chunks % 2 == 1:
                m, acc = absorb(chunks - 1, n, s_a, m_a, (m, acc))
            o = acc[:dh] / acc[dh:dh + 1]
            o_ref[hf * tq:(hf + 1) * tq, g * dh:(g + 1) * dh] = o.T.astype(o_ref.dtype)

    i = pl.program_id(1)

    @pl.when(i < ctx_steps)
    def _():
        run(-(-ctx_len // tk), True)

    @pl.when(i >= ctx_steps)
    def _():
        run(n_chunks, False)


def flash_attention(q_t, k, v_t, *, tq, ctx_len, halves=2):
    heads, dh, t = q_t.shape
    kv, n_chunks, rows, tk = v_t.shape
    group = heads // kv
    step = halves * tq
    pad = -ctx_len % step
    assert ctx_len % tq == 0 and (pad + t) % step == 0
    q_t = jnp.pad(q_t, ((0, 0), (0, 0), (pad, 0)))
    out = pl.pallas_call(
        functools.partial(_flash_body, tq=tq, tk=tk, group=group, halves=halves, ctx_len=ctx_len,
                          ctx_steps=(pad + ctx_len) // step, n_chunks=n_chunks),
        grid=(kv, (pad + t) // step),
        in_specs=[
            pl.BlockSpec((group, dh, step), lambda h, i: (h, 0, i)),
            pl.BlockSpec((None, t, dh), lambda h, i: (h, 0, 0)),
            pl.BlockSpec((None, n_chunks, rows, tk), lambda h, i: (h, 0, 0, 0)),
        ],
        out_specs=pl.BlockSpec((step, group * dh), lambda h, i: (i, h)),
        out_shape=jax.ShapeDtypeStruct((pad + t, heads * dh), BF16),
        scratch_shapes=[pltpu.VMEM((halves * group, tk, tq), F32), pltpu.VMEM((halves * group, tk, tq), F32),
                        pltpu.VMEM((halves * group, 1, tq), F32), pltpu.VMEM((halves * group, 1, tq), F32)],
        compiler_params=_params("parallel", "parallel"),
        name="flash_attention",
    )(q_t, k, v_t)
    return out[pad:]


def _convpool_body(ah_ref, ab_ref, ac_ref, pu_ref, ah_up, ac_up, pu_up, ah_dn, ac_dn, pu_dn,
                   cw_ref, pw_ref, ps_ref, yc_ref, yp_ref, xbuf, pbuf,
                   *, tr, ctx_tiles, n_tiles, ctx_len, seq_len):
    i = pl.program_id(0)
    has_up = jnp.logical_and(i != 0, i != ctx_tiles)
    has_dn = jnp.logical_and(i != ctx_tiles - 1, i != n_tiles - 1)

    f32 = lambda ref: ref[...].astype(F32)
    xbuf[0:HALO] = jnp.where(has_up, f32(ac_up) * f32(ah_up), 0.0)
    xbuf[HALO:HALO + tr] = f32(ac_ref) * f32(ah_ref)
    xbuf[HALO + tr:HALO + tr + HALO] = jnp.where(has_dn, f32(ac_dn) * f32(ah_dn), 0.0)
    conv = (cw_ref[0:1] * xbuf[HALO - 1:HALO - 1 + tr] + cw_ref[1:2] * xbuf[HALO:HALO + tr]
            + cw_ref[2:3] * xbuf[HALO + 1:HALO + 1 + tr])
    yc_ref[...] = (f32(ab_ref) * conv).astype(yc_ref.dtype)

    pbuf[0:HALO] = jnp.where(has_up, f32(pu_up), 0.0)
    pbuf[HALO:HALO + tr] = f32(pu_ref)
    pbuf[HALO + tr:HALO + tr + HALO] = jnp.where(has_dn, f32(pu_dn), 0.0)
    in_ctx = i < ctx_tiles
    pos = lax.broadcasted_iota(jnp.int32, (tr, 1), 0) + jnp.where(in_ctx, i, i - ctx_tiles) * tr
    length = jnp.where(in_ctx, ctx_len, seq_len)
    pg = pw_ref.shape[-1]
    for g, win in enumerate(POOL_WINDOWS):
        cols = slice(g * pg, (g + 1) * pg)
        total = None
        for s in range(-(win // 2), win - win // 2):
            piece = pbuf[HALO + s:HALO + s + tr, cols]
            total = piece if total is None else total + piece
        count = jnp.minimum(pos - win // 2 + win, length) - jnp.maximum(pos - win // 2, 0)
        pooled = total / count.astype(F32) - pbuf[HALO:HALO + tr, cols]
        y = jnp.dot(pooled.astype(BF16), pw_ref[g], preferred_element_type=F32) * ps_ref[:, cols]
        yp_ref[:, cols] = y.astype(yp_ref.dtype)


def conv_pool(p, conv_w, pool_w, pool_scale, layer, *, width, tr, ctx_len, seq_len):
    t = p.shape[0]
    n_tiles = t // tr
    ctx_tiles = ctx_len // tr
    assert ctx_len % tr == 0 and tr % HALO == 0 and max(POOL_WINDOWS) // 2 <= HALO
    hb = tr // HALO
    main = lambda c: pl.BlockSpec((tr, width), lambda i: (i, c))
    up = lambda c: pl.BlockSpec((HALO, width), lambda i: (jnp.maximum(i * hb - 1, 0), c))
    dn = lambda c: pl.BlockSpec((HALO, width), lambda i: (jnp.minimum((i + 1) * hb, t // HALO - 1), c))
    return pl.pallas_call(
        functools.partial(_convpool_body, tr=tr, ctx_tiles=ctx_tiles, n_tiles=n_tiles,
                          ctx_len=ctx_len, seq_len=seq_len),
        grid=(n_tiles,),
        in_specs=[main(0), main(1), main(2), main(4), up(0), up(2), up(4), dn(0), dn(2), dn(4),
                  pl.BlockSpec((None,) + conv_w.shape[1:], lambda i: (layer, 0, 0)),
                  pl.BlockSpec((None,) + pool_w.shape[1:], lambda i: (layer, 0, 0, 0)),
                  pl.BlockSpec((None, 1, width), lambda i: (layer, 0, 0))],
        out_specs=[pl.BlockSpec((tr, width), lambda i: (i, 0))] * 2,
        out_shape=[jax.ShapeDtypeStruct((t, width), BF16)] * 2,
        scratch_shapes=[pltpu.VMEM((tr + 2 * HALO, width), F32)] * 2,
        compiler_params=_params("parallel"),
        name="conv_pool",
    )(p, p, p, p, p, p, p, p, p, p, conv_w, pool_w, pool_scale.reshape(-1, 1, width))


def s5_chunk_weights(lam_re, lam_im, log_step, b_re, b_im, c_re, c_im):
    tc = S5_CHUNK
    lam = lax.complex(lam_re.astype(F32), lam_im.astype(F32))
    z = lam * jnp.exp(log_step.astype(F32))[..., None]
    lam_bar = jnp.exp(z)
    b_bar = ((lam_bar - 1.0) / lam)[..., None] * lax.complex(b_re.astype(F32), b_im.astype(F32))
    c = lax.complex(c_re.astype(F32), c_im.astype(F32))
    depth, _, groups, p_dim, s_dim = b_bar.shape
    powers = jnp.exp(z[..., None, :] * jnp.arange(tc + 1, dtype=F32)[:, None])

    cp = c[..., None, :, :] * powers[..., :tc, None, :]
    kern = (jnp.einsum('ldgtop,ldgpi->ldgtoi', jnp.real(cp), jnp.real(b_bar), precision=HIGHEST)
            - jnp.einsum('ldgtop,ldgpi->ldgtoi', jnp.imag(cp), jnp.imag(b_bar), precision=HIGHEST))
    k_f, k_b = kern[:, 0], kern[:, 1]
    by_lag = jnp.concatenate([k_b[:, :, :0:-1], k_f[:, :, :1] + k_b[:, :, :1], k_f[:, :, 1:],
                              jnp.zeros_like(k_f[:, :, :1])], axis=2)
    skew = jnp.tile(by_lag, (1, 1, tc, 1, 1))[:, :, :tc * (2 * tc - 1)]
    skew = skew.reshape(depth, groups, tc, 2 * tc - 1, s_dim, s_dim)[:, :, :, tc - 1:]
    toeplitz = jnp.transpose(skew, (0, 1, 2, 5, 3, 4)).reshape(depth, groups, tc * s_dim, tc * s_dim)

    f_f = c[:, 0][:, :, None] * powers[:, 0, :, 1:tc + 1, None, :]
    f_b = c[:, 1][:, :, None] * powers[:, 1, :, tc:0:-1, None, :]
    rows = lambda a: jnp.transpose(a, (0, 1, 4, 2, 3)).reshape(depth, groups, p_dim, tc * s_dim)
    w_out = jnp.concatenate([toeplitz, rows(jnp.real(f_f)), rows(jnp.real(f_b)),
                             rows(-jnp.imag(f_f)), rows(-jnp.imag(f_b))], axis=2)

    e_f = powers[:, 0, :, tc - 1::-1][:, :, :tc, :, None] * b_bar[:, 0][:, :, None]
    e_b = powers[:, 1, :, :tc, :, None] * b_bar[:, 1][:, :, None]
    cols = lambda a: jnp.transpose(a, (0, 1, 2, 4, 3)).reshape(depth, groups, tc * s_dim, p_dim)
    w_state = jnp.concatenate([cols(jnp.real(e_f)), cols(jnp.real(e_b)),
                               cols(jnp.imag(e_f)), cols(jnp.imag(e_b))], axis=3)

    last = powers[..., tc, :]
    decay = jnp.stack([jnp.concatenate([jnp.real(last[:, 0]), jnp.real(last[:, 1])], axis=-1),
                       jnp.concatenate([jnp.imag(last[:, 0]), jnp.imag(last[:, 1])], axis=-1)], axis=2)
    return _slab_operators(w_state, w_out, decay, tc, s_dim)


def _slab_operators(w_state, w_out, decay, tc, s_dim):
    depth, groups, cs, st = w_state.shape
    gl = LANES // s_dim
    k = groups // gl
    p_dim = st // 4
    w_state, w_out = lax.optimization_barrier((w_state.astype(BF16), w_out.astype(BF16)))

    def stacked_rows(a, piece):
        rows, cols = a.shape[2:]
        a = a.reshape(depth, k, gl, rows // piece, piece, cols)
        return jnp.transpose(a, (0, 1, 3, 2, 4, 5)).reshape(depth, k, gl * rows, cols).astype(BF16)

    def spread(cols, piece):
        c = np.arange(gl * cols)
        src = c // (gl * piece) * piece + c % piece
        return jnp.asarray(np.arange(cols)[:, None] == src[None, :], dtype=BF16)

    ws = expand_block_diagonal(stacked_rows(w_state, s_dim), spread(st, p_dim), s_dim, p_dim, gl)
    top = expand_block_diagonal(stacked_rows(w_out[:, :, :cs], s_dim), spread(cs, s_dim), s_dim, s_dim, gl)
    bot = expand_block_diagonal(stacked_rows(w_out[:, :, cs:], p_dim), spread(cs, s_dim), p_dim, s_dim, gl)
    decay_s = jnp.transpose(decay.reshape(depth, k, gl, 2, 2, p_dim), (0, 1, 3, 4, 2, 5))
    return ws, top, bot, decay_s.reshape(depth, k, 2, 2 * gl * p_dim)


def _expand_body(a_ref, c_ref, o_ref, *, row_piece, col_piece, gl):
    r = jnp.dot(a_ref[...], c_ref[...], preferred_element_type=F32)
    row0 = pl.program_id(2) * a_ref.shape[0]
    row_group = (row0 + lax.broadcasted_iota(jnp.int32, (a_ref.shape[0], 1), 0)) // row_piece % gl
    col_group = lax.broadcasted_iota(jnp.int32, (1, c_ref.shape[1]), 1) // col_piece % gl
    o_ref[...] = jnp.where(row_group == col_group, r, 0.0).astype(o_ref.dtype)


def expand_block_diagonal(a, spread, row_piece, col_piece, gl):
    depth, k, rows, cols = a.shape
    tr = _tile(rows, 512, 16)
    return pl.pallas_call(
        functools.partial(_expand_body, row_piece=row_piece, col_piece=col_piece, gl=gl),
        grid=(depth, k, rows // tr),
        in_specs=[pl.BlockSpec((None, None, tr, cols), lambda l, i, r: (l, i, r, 0)),
                  pl.BlockSpec(spread.shape, lambda l, i, r: (0, 0))],
        out_specs=pl.BlockSpec((None, None, tr, gl * cols), lambda l, i, r: (l, i, r, 0)),
        out_shape=jax.ShapeDtypeStruct((depth, k, rows, gl * cols), BF16),
        compiler_params=_params("parallel", "parallel", "parallel"),
        name="expand_block_diagonal",
    )(a, spread)


def _s5_state_body(u_ref, w_ref, decay_ref, h_ref, s_scr, *, n, ctx_chunks):
    half = decay_ref.shape[-1]
    q = half // 2
    rows = n // 2 if n % 16 == 0 else n
    for r in range(0, n, rows):
        s_scr[r:r + rows] = jnp.dot(u_ref[r:r + rows], w_ref[...], preferred_element_type=F32)
    parts = ((slice(0, q), slice(half, half + q)), (slice(q, half), slice(half + q, 2 * half)))
    decays = [(decay_ref[0:1, lo:lo + q], decay_ref[1:2, lo:lo + q]) for lo in (0, q)]

    def scan(j, carry):
        jb = jnp.where(j < ctx_chunks, ctx_chunks - 1 - j, n - 1 + ctx_chunks - j)
        out = []
        for row, (re, im), (a_re, a_im), (c_re, c_im) in zip((j, jb), parts, decays, carry):
            s_re, s_im = s_scr[pl.ds(row, 1), re], s_scr[pl.ds(row, 1), im]
            s_scr[pl.ds(row, 1), re] = c_re
            s_scr[pl.ds(row, 1), im] = c_im
            out.append((a_re * c_re - a_im * c_im + s_re, a_re * c_im + a_im * c_re + s_im))
        return tuple(out)

    zero = jnp.zeros((1, q), F32)
    lax.fori_loop(0, n, scan, ((zero, zero), (zero, zero)))
    h_ref[...] = s_scr[...].astype(h_ref.dtype)


def s5_state(u, w_state_s, decay_s, layer, *, ctx_chunks):
    k, n, cs = u.shape
    st = w_state_s.shape[-1]
    return pl.pallas_call(
        functools.partial(_s5_state_body, n=n, ctx_chunks=ctx_chunks),
        grid=(k,),
        in_specs=[
            pl.BlockSpec((None, n, cs), lambda i: (i, 0, 0)),
            pl.BlockSpec((None, None, cs, st), lambda i: (layer, i, 0, 0)),
            pl.BlockSpec((None, None, 2, st // 2), lambda i: (layer, i, 0, 0)),
        ],
        out_specs=pl.BlockSpec((None, n, st), lambda i: (i, 0, 0)),
        out_shape=jax.ShapeDtypeStruct((k, n, st), BF16),
        scratch_shapes=[pltpu.VMEM((n, st), F32)],
        compiler_params=_params("parallel"),
        name="s5_state",
    )(u, w_state_s, decay_s)


def _s5_readout_body(u_ref, h_ref, wu_ref, wh_ref, y_ref):
    y_ref[...] = (jnp.dot(u_ref[...], wu_ref[...], preferred_element_type=F32)
                  + jnp.dot(h_ref[...], wh_ref[...], preferred_element_type=F32))


def s5_readout(u, h, w_in_chunk, w_from_state, layer, *, tn):
    k, n, cs = u.shape
    st = h.shape[-1]
    return pl.pallas_call(
        _s5_readout_body,
        grid=(k, cs // tn),
        in_specs=[
            pl.BlockSpec((None, n, cs), lambda i, j: (i, 0, 0)),
            pl.BlockSpec((None, n, st), lambda i, j: (i, 0, 0)),
            pl.BlockSpec((None, None, cs, tn), lambda i, j: (layer, i, 0, j)),
            pl.BlockSpec((None, None, st, tn), lambda i, j: (layer, i, 0, j)),
        ],
        out_specs=pl.BlockSpec((None, n, tn), lambda i, j: (i, 0, j)),
        out_shape=jax.ShapeDtypeStruct((k, n, cs), F32),
        compiler_params=_params("parallel", "parallel"),
        name="s5_readout",
    )(u, h, w_in_chunk, w_from_state)


def _s5_glu_body(y_ref, u_ref, d_ref, w_ref, o_ref):
    z = jax.nn.gelu(y_ref[...] + d_ref[...] * u_ref[...].astype(F32))
    gate = _sigmoid(jnp.dot(z.astype(BF16), w_ref[...], preferred_element_type=F32))
    o_ref[...] = (z * gate).astype(o_ref.dtype)


def s5_glu(y, p, d, w_glu, layer, *, width, tr):
    t = p.shape[0]
    return pl.pallas_call(
        _s5_glu_body,
        grid=(t // tr,),
        in_specs=[
            pl.BlockSpec((tr, width), lambda i: (i, 0)),
            pl.BlockSpec((tr, width), lambda i: (i, 3)),
            pl.BlockSpec((None, 1, width), lambda i: (layer, 0, 0)),
            pl.BlockSpec((None, width, width), lambda i: (layer, 0, 0)),
        ],
        out_specs=pl.BlockSpec((tr, width), lambda i: (i, 0)),
        out_shape=jax.ShapeDtypeStruct((t, width), BF16),
        compiler_params=_params("parallel"),
        name="s5_glu",
    )(y, p, d.reshape(-1, 1, width), w_glu)


def _merge_body(gl_ref, yc_ref, ys_ref, yp_ref, ya_ref, gu_ref, gb_ref, wc_ref, ws_ref, wp_ref, wa_ref,
                o_ref):
    g_low = gl_ref[...]
    total = None
    for b, (y_ref, w_ref) in enumerate(((yc_ref, wc_ref), (ys_ref, ws_ref), (yp_ref, wp_ref),
                                        (ya_ref, wa_ref))):
        gate = _sigmoid(jnp.dot(g_low, gu_ref[b], preferred_element_type=F32) + gb_ref[b])
        term = gate * jnp.dot(y_ref[...], w_ref[...], preferred_element_type=F32)
        total = term if total is None else total + term
    o_ref[...] = total.astype(o_ref.dtype)


def merge(p, ys, gate_up, gate_bias, w_outs, layer, *, gate_col, tm, tn):
    t = p.shape[0]
    _, nb, rank, d = gate_up.shape
    row = lambda a: pl.BlockSpec((tm, a.shape[1]), lambda i, j: (i, 0))
    col = lambda a: pl.BlockSpec((None, a.shape[1], tn), lambda i, j: (layer, 0, j))
    return pl.pallas_call(
        _merge_body,
        grid=(t // tm, d // tn),
        in_specs=[pl.BlockSpec((tm, rank), lambda i, j: (i, gate_col))] + [row(y) for y in ys]
                 + [pl.BlockSpec((None, nb, rank, tn), lambda i, j: (layer, 0, 0, j)),
                    pl.BlockSpec((None, nb, 1, tn), lambda i, j: (layer, 0, 0, j))] + [col(w) for w in w_outs],
        out_specs=pl.BlockSpec((tm, tn), lambda i, j: (i, j)),
        out_shape=jax.ShapeDtypeStruct((t, d), BF16),
        compiler_params=_params("parallel", "parallel"),
        name="merge",
    )(p, *ys, gate_up, gate_bias.reshape(-1, nb, 1, d), *w_outs)


def _rope_tables(ctx_len, seq_len, dh):
    pairs = dh // 4
    rows = seq_len // GRID_W
    row = jnp.repeat(jnp.arange(rows), GRID_W).astype(F32)
    col = jnp.tile(jnp.arange(GRID_W), rows).astype(F32)
    inv = ROPE_THETA ** (-jnp.arange(pairs, dtype=F32) / pairs)
    ang = jnp.concatenate([row[:, None] * inv, col[:, None] * inv], axis=-1)
    cos = jnp.repeat(jnp.cos(ang), 2, axis=-1)
    sin = jnp.stack([-jnp.sin(ang), jnp.sin(ang)], axis=-1).reshape(seq_len, dh)
    cos = jnp.concatenate([jnp.ones((ctx_len, dh), F32), cos], axis=0)
    sin = jnp.concatenate([jnp.zeros((ctx_len, dh), F32), sin], axis=0)
    return cos, sin


def kernel(x, c, ctx, c_ctx, ada_down, ada_up, ada_bias, norm_gains, w_in, conv_w, s5_lambda_re, s5_lambda_im, s5_log_step, s5_b_re, s5_b_im, s5_c_re, s5_c_im, s5_d, s5_w_glu, pool_w, pool_scale, qk_norm, gate_up, gate_bias, w_out_conv, w_out_s5, w_out_pool, w_out_attn, w_o, ffn_w_gate, ffn_w_up, ffn_w_down):
    batch, seq_len, d = x.shape
    ctx_len = ctx.shape[1]
    depth = w_in.shape[0]
    assert batch == 1 and c.shape[0] == 1
    t = ctx_len + seq_len
    width = conv_w.shape[-1]
    assert s5_d.shape[-1] == width and pool_scale.shape[-1] == width
    dh = qk_norm.shape[-1]
    q_width = w_out_attn.shape[1]
    rank = gate_up.shape[2]
    in_width = w_in.shape[-1]
    kv_width = (in_width - 5 * width - q_width - rank) // 2
    heads, kv_heads = q_width // dh, kv_width // dh
    groups, s_dim = s5_b_re.shape[2], s5_b_re.shape[-1]
    slabs = width // LANES
    assert groups * s_dim == width and LANES % s_dim == 0
    hidden = ffn_w_gate.shape[-1]
    q_col, k_col, v_col = 5 * width, 5 * width + q_width, 5 * width + q_width + kv_width
    gate_col = (v_col + kv_width) // rank
    assert q_col % dh == 0 and (v_col + kv_width) % rank == 0 and width % dh == 0

    tr = _tile(ctx_len, 256, 8)
    tm = _tile(t, 1280, 256)
    tm_half = _tile(t, 640, 128)
    tk_att = _tile(t, 1280, 256)
    tq = _tile(ctx_len, 256, 128)
    ctx_tiles = ctx_len // tr

    w_in_b, w_o_b, w_gate_b, w_up_b, w_down_b = (w.astype(BF16) for w in (w_in, w_o, ffn_w_gate, ffn_w_up,
                                                                         ffn_w_down))
    w_outs_b = [w.astype(BF16) for w in (w_out_conv, w_out_s5, w_out_pool, w_out_attn)]
    gate_up_b, pool_w_b, w_glu_b = gate_up.astype(BF16), pool_w.astype(BF16), s5_w_glu.astype(BF16)
    w_state, w_s5top, w_s5bot, decay = s5_chunk_weights(s5_lambda_re, s5_lambda_im, s5_log_step,
                                               s5_b_re, s5_b_im, s5_c_re, s5_c_im)
    cos, sin = _rope_tables(ctx_len, seq_len, dh)
    qk_gains = qk_norm.reshape(depth, 2, 1, dh)

    cond = jnp.concatenate([c_ctx[None, :], c, jnp.zeros((6, d), F32)], axis=0)
    mods = ada_modulation(cond, ada_down, ada_up, ada_bias)
    mods = mods[:, :2].reshape(depth, 2, N_MOD, d)
    mods = jnp.concatenate([mods, jnp.zeros((depth, 2, 8 - N_MOD, d), F32)], axis=2)

    stream = jnp.concatenate([ctx[0], x[0]], axis=0)
    row_kw = dict(ctx_tiles=ctx_tiles, tr=tr)
    h = modulate(stream, mods, norm_gains, 0, rows=(0, 1, 0), **row_kw)
    n_chunks = t // S5_CHUNK
    for l in range(depth):
        p = matmul(h, w_in_b, l, tm=tm, tn=_tile(in_width, 1024, 256), out_dtype=BF16)

        q_t = qk_prep(p, qk_gains, cos, sin, l, 0, col0=q_col // dh, heads=heads, tr=tk_att,
                      scale=dh ** -0.5 * math.log2(math.e), transpose=True)
        k_n = qk_prep(p, qk_gains, cos, sin, l, 1, col0=k_col // dh, heads=kv_heads, tr=tk_att,
                      scale=1.0, transpose=False)
        v_t = v_prep(p, col0=v_col // dh, heads=kv_heads, dh=dh, tk=tk_att)
        y_attn = flash_attention(q_t, k_n, v_t, tq=tq, ctx_len=ctx_len)

        y_conv, y_pool = conv_pool(p, conv_w, pool_w_b, pool_scale, l, width=width, tr=tr,
                                   ctx_len=ctx_len, seq_len=seq_len)

        u = p[:, 3 * width:4 * width].reshape(n_chunks, S5_CHUNK, slabs, LANES)
        u = jnp.transpose(u, (2, 0, 1, 3)).reshape(slabs, n_chunks, S5_CHUNK * LANES)
        h_s5 = s5_state(u, w_state, decay, l, ctx_chunks=ctx_len // S5_CHUNK)
        y_s5 = s5_readout(u, h_s5, w_s5top, w_s5bot, l, tn=_tile(S5_CHUNK * LANES, 1024, 256))
        y_s5 = jnp.transpose(y_s5.reshape(slabs, n_chunks, S5_CHUNK, LANES), (1, 2, 0, 3)).reshape(t, width)
        y_s5 = s5_glu(y_s5, p, s5_d, w_glu_b, l, width=width, tr=tm_half)

        merged = merge(p, [y_conv, y_s5, y_pool, y_attn], gate_up_b, gate_bias, w_outs_b, l,
                       gate_col=gate_col, tm=tm, tn=_tile(d, 256, 256))
        mix = matmul(merged, w_o_b, l, tm=tm, tn=_tile(d, 1024, 256), out_dtype=BF16)
        stream, h = residual(stream, mix, mods, norm_gains, l, gate_row=2, post_row=1, nxt=(3, 4, 2), **row_kw)

        hid = ffn_up(h, w_gate_b, w_up_b, l, tm=_tile(t, 1664, 128), tn=_tile(hidden, 256, 128))
        ffn = matmul(hid, w_down_b, l, tm=tm_half, tn=_tile(d, 1024, 256), nk=2)
        if l + 1 < depth:
            stream, h = residual(stream, ffn, mods, norm_gains, l, gate_row=5, post_row=3, nxt=(0, 1, 0),
                                 next_layer=l + 1, **row_kw)
        else:
            stream, _ = residual(stream, ffn, mods, norm_gains, l, gate_row=5, post_row=3,
                                 first_tile=ctx_tiles, **row_kw)
    return stream[None]
```

```python
import functools
import math

import jax
import jax.numpy as jnp
import numpy as np
from jax import lax
from jax.experimental import pallas as pl
from jax.experimental.pallas import tpu as pltpu

EPS = 1e-6
GRID_W = 64
ROPE_THETA = 10000.0
POOL_WINDOWS = (2, 4, 8, 16)
N_MOD = 6
S5_CHUNK = 16
HALO = 16
LANES = 128
ONES_ROWS = 16
NEG_BIG = -1e30
V7X_VMEM_LIMIT = 56 * 1024 * 1024
F32 = jnp.float32
BF16 = jnp.bfloat16
HIGHEST = lax.Precision.HIGHEST


def _params(*sem):
    return pltpu.CompilerParams(dimension_semantics=sem, vmem_limit_bytes=V7X_VMEM_LIMIT)


def _tile(n, target, mult):
    best = None
    for t in range(mult, min(n, target) + 1, mult):
        if n % t == 0:
            best = t
    assert best is not None, (n, target, mult)
    return best


def _rms(x, gain):
    return x * lax.rsqrt(jnp.mean(x * x, axis=-1, keepdims=True) + EPS) * gain


def _sigmoid(x):
    return 1.0 / (1.0 + jnp.exp(-x))


def _ada_body(cond_ref, down_ref, up_ref, bias_ref, o_ref):
    s = cond_ref[...]
    s = s * _sigmoid(s)
    low = jnp.dot(s, down_ref[...], precision=HIGHEST, preferred_element_type=F32)
    o_ref[...] = jnp.dot(low, up_ref[...], precision=HIGHEST, preferred_element_type=F32) + bias_ref[...]


def ada_modulation(cond, down, up, bias):
    depth, d, rank = down.shape
    width = up.shape[-1]
    tn = _tile(width, 4096, 128)
    return pl.pallas_call(
        _ada_body,
        grid=(depth, width // tn),
        in_specs=[
            pl.BlockSpec((8, d), lambda l, j: (0, 0)),
            pl.BlockSpec((None, d, rank), lambda l, j: (l, 0, 0)),
            pl.BlockSpec((None, rank, tn), lambda l, j: (l, 0, j)),
            pl.BlockSpec((None, 1, tn), lambda l, j: (l, 0, j)),
        ],
        out_specs=pl.BlockSpec((None, 8, tn), lambda l, j: (l, 0, j)),
        out_shape=jax.ShapeDtypeStruct((depth, 8, width), F32),
        compiler_params=_params("arbitrary", "arbitrary"),
        name="ada_modulation",
    )(cond, down, up, bias.reshape(depth, 1, width))


def _modulated(x, mod, gains, rows):
    shift_row, scale_row, gain_row = rows
    h = _rms(x, gains[gain_row:gain_row + 1])
    return h * (1.0 + mod[scale_row:scale_row + 1]) + mod[shift_row:shift_row + 1]


def _modulate_body(x_ref, mod_ref, gain_ref, h_ref, *, rows):
    h_ref[...] = _modulated(x_ref[...], mod_ref[...], gain_ref[...], rows).astype(h_ref.dtype)


def modulate(x, mods, gains, layer, *, ctx_tiles, tr, rows):
    t, d = x.shape
    return pl.pallas_call(
        functools.partial(_modulate_body, rows=rows),
        grid=(t // tr,),
        in_specs=[
            pl.BlockSpec((tr, d), lambda i: (i, 0)),
            pl.BlockSpec((None, None, 8, d), lambda i: (layer, jnp.where(i < ctx_tiles, 0, 1), 0, 0)),
            pl.BlockSpec((None,) + gains.shape[1:], lambda i: (layer, 0, 0)),
        ],
        out_specs=pl.BlockSpec((tr, d), lambda i: (i, 0)),
        out_shape=jax.ShapeDtypeStruct((t, d), BF16),
        compiler_params=_params("parallel"),
        name="modulate",
    )(x, mods, gains)


def _residual_body(x_ref, y_ref, mod_ref, gain_ref, modn_ref, gainn_ref, xo_ref, *h_ref,
                   gate_row, post_row, nxt):
    mod = mod_ref[...]
    xn = x_ref[...] + mod[gate_row:gate_row + 1] * _rms(y_ref[...].astype(F32), gain_ref[post_row:post_row + 1])
    xo_ref[...] = xn
    if nxt is not None:
        h_ref[0][...] = _modulated(xn, modn_ref[...], gainn_ref[...], nxt).astype(h_ref[0].dtype)


def residual(x, y, mods, gains, layer, *, ctx_tiles, tr, gate_row, post_row, nxt=None, next_layer=None,
             first_tile=0):
    t, d = x.shape
    n_tiles = t // tr - first_tile
    if next_layer is None:
        next_layer = layer
    who = lambda i: jnp.where(i + first_tile < ctx_tiles, 0, 1)
    row_in = pl.BlockSpec((tr, d), lambda i: (i + first_tile, 0))
    row_out = pl.BlockSpec((tr, d), lambda i: (i, 0))
    mod_spec = lambda l: pl.BlockSpec((None, None, 8, d), lambda i: (l, who(i), 0, 0))
    gain_spec = lambda l: pl.BlockSpec((None,) + gains.shape[1:], lambda i: (l, 0, 0))
    out_specs = [row_out]
    out_shape = [jax.ShapeDtypeStruct((n_tiles * tr, d), F32)]
    if nxt is not None:
        out_specs.append(row_out)
        out_shape.append(jax.ShapeDtypeStruct((n_tiles * tr, d), BF16))
    out = pl.pallas_call(
        functools.partial(_residual_body, gate_row=gate_row, post_row=post_row, nxt=nxt),
        grid=(n_tiles,),
        in_specs=[row_in, row_in, mod_spec(layer), gain_spec(layer), mod_spec(next_layer),
                  gain_spec(next_layer)],
        out_specs=out_specs,
        out_shape=out_shape,
        compiler_params=_params("parallel"),
        name="residual_norm",
    )(x, y, mods, gains, mods, gains)
    return out if nxt is not None else (out[0], None)


def _matmul_body(a_ref, b_ref, o_ref):
    o_ref[...] = jnp.dot(a_ref[...], b_ref[...], preferred_element_type=F32).astype(o_ref.dtype)


def matmul(a, b, layer, *, tm, tn, out_dtype=BF16):
    m, k = a.shape
    n = b.shape[-1]
    return pl.pallas_call(
        _matmul_body,
        grid=(m // tm, n // tn),
        in_specs=[
            pl.BlockSpec((tm, k), lambda i, j: (i, 0)),
            pl.BlockSpec((None, k, tn), lambda i, j: (layer, 0, j)),
        ],
        out_specs=pl.BlockSpec((tm, tn), lambda i, j: (i, j)),
        out_shape=jax.ShapeDtypeStruct((m, n), out_dtype),
        compiler_params=_params("parallel", "parallel"),
        name="matmul",
    )(a, b)


def _ffn_up_body(a_ref, wg_ref, wu_ref, o_ref):
    a = a_ref[...]
    g = jnp.dot(a, wg_ref[...], preferred_element_type=F32)
    u = jnp.dot(a, wu_ref[...], preferred_element_type=F32)
    o_ref[...] = (g * _sigmoid(g) * u).astype(o_ref.dtype)


def ffn_up(a, w_gate, w_up, layer, *, tm, tn):
    m, k = a.shape
    n = w_gate.shape[-1]
    w_spec = pl.BlockSpec((None, k, tn), lambda i, j: (layer, 0, j))
    return pl.pallas_call(
        _ffn_up_body,
        grid=(m // tm, n // tn),
        in_specs=[pl.BlockSpec((tm, k), lambda i, j: (i, 0)), w_spec, w_spec],
        out_specs=pl.BlockSpec((tm, tn), lambda i, j: (i, j)),
        out_shape=jax.ShapeDtypeStruct((m, n), BF16),
        compiler_params=_params("parallel", "parallel"),
        name="ffn_up",
    )(a, w_gate, w_up)


def _qk_prep_body(x_ref, gain_ref, cos_ref, sin_ref, o_ref, *, scale, transpose):
    y = _rms(x_ref[...].astype(F32), gain_ref[...])
    lanes = y.shape[-1]
    even = lax.broadcasted_iota(jnp.int32, (1, lanes), 1) % 2 == 0
    partner = jnp.where(even, pltpu.roll(y, lanes - 1, 1), pltpu.roll(y, 1, 1))
    y = (y * cos_ref[...] + partner * sin_ref[...]) * scale
    if transpose:
        o_ref[...] = y.T.astype(o_ref.dtype)
    else:
        o_ref[...] = y.astype(o_ref.dtype)


def qk_prep(p, gains, cos, sin, layer, which, *, col0, heads, tr, scale, transpose):
    t = p.shape[0]
    dh = gains.shape[-1]
    if transpose:
        out_spec = pl.BlockSpec((None, dh, tr), lambda i, h: (h, 0, i))
        out_shape = jax.ShapeDtypeStruct((heads, dh, t), BF16)
    else:
        out_spec = pl.BlockSpec((None, tr, dh), lambda i, h: (h, i, 0))
        out_shape = jax.ShapeDtypeStruct((heads, t, dh), BF16)
    return pl.pallas_call(
        functools.partial(_qk_prep_body, scale=scale, transpose=transpose),
        grid=(t // tr, heads),
        in_specs=[
            pl.BlockSpec((tr, dh), lambda i, h: (i, col0 + h)),
            pl.BlockSpec((None, None, 1, dh), lambda i, h: (layer, which, 0, 0)),
            pl.BlockSpec((tr, dh), lambda i, h: (i, 0)),
            pl.BlockSpec((tr, dh), lambda i, h: (i, 0)),
        ],
        out_specs=out_spec,
        out_shape=out_shape,
        compiler_params=_params("parallel", "parallel"),
        name="qk_prep",
    )(p, gains, cos, sin)


def _v_prep_body(x_ref, o_ref):
    dh = x_ref.shape[1]
    o_ref[:dh] = x_ref[...].astype(F32).T.astype(o_ref.dtype)
    first = lax.broadcasted_iota(jnp.int32, (ONES_ROWS, o_ref.shape[1]), 0) == 0
    o_ref[dh:] = jnp.where(first, 1.0, 0.0).astype(o_ref.dtype)


def v_prep(p, *, col0, heads, dh, tk):
    t = p.shape[0]
    return pl.pallas_call(
        _v_prep_body,
        grid=(t // tk, heads),
        in_specs=[pl.BlockSpec((tk, dh), lambda i, h: (i, col0 + h))],
        out_specs=pl.BlockSpec((None, None, dh + ONES_ROWS, tk), lambda i, h: (h, i, 0, 0)),
        out_shape=jax.ShapeDtypeStruct((heads, t // tk, dh + ONES_ROWS, tk), BF16),
        compiler_params=_params("parallel", "parallel"),
        name="v_prep",
    )(p)


def _flash_body(q_ref, k_ref, v_ref, o_ref, s_a, s_b, m_a, m_b,
                *, tq, tk, group, halves, ctx_len, ctx_steps, n_chunks):
    dh = q_ref.shape[1]
    rows = v_ref.shape[1]
    streams = [(g, hf) for hf in range(halves) for g in range(group)]

    def scores(c, n, s_ref, m_ref, masked):
        g, hf = streams[n]
        k_c = k_ref[pl.ds(pl.multiple_of(c * tk, tk), tk), :]
        s = jnp.dot(k_c, q_ref[g, :, hf * tq:(hf + 1) * tq], preferred_element_type=F32)
        if masked:
            key = c * tk + lax.broadcasted_iota(jnp.int32, (tk, 1), 0)
            s = jnp.where(key < ctx_len, s, NEG_BIG)
        s_ref[n] = s
        m_ref[n] = jnp.max(s, axis=0, keepdims=True)

    def absorb(c, n, s_ref, m_ref, carry):
        m, acc = carry
        m_new = jnp.maximum(m, m_ref[n])
        p = jnp.exp2(s_ref[n] - m_new).astype(BF16)
        acc = jnp.exp2(m - m_new) * acc + jnp.dot(v_ref[c], p, preferred_element_type=F32)
        return m_new, acc

    def run(chunks, masked):
        ns = len(streams)
        carry = tuple((jnp.full((1, tq), NEG_BIG, F32), jnp.zeros((rows, tq), F32)) for _ in range(ns))
        for n in range(ns):
            scores(0, n, s_a, m_a, masked)

        def half(c_next, c_cur, nxt, cur, carry):
            out = []
            for n in range(ns):
                scores(c_next, n, *nxt, masked)
                out.append(absorb(c_cur, n, *cur, carry[n]))
            return tuple(out)

        def pair(c, carry):
            carry = half(c + 1, c, (s_b, m_b), (s_a, m_a), carry)
            return half(jnp.minimum(c + 2, chunks - 1), c + 1, (s_a, m_a), (s_b, m_b), carry)

        def quad(i, carry):
            return pair(4 * i + 2, pair(4 * i, carry))

        carry = lax.fori_loop(0, chunks // 4, quad, carry)
        if chunks % 4 >= 2:
            carry = pair(chunks // 4 * 4, carry)
        for n, (g, hf) in enumerate(streams):
            m, acc = carry[n]
            if chunks % 2 == 1:
                m, acc = absorb(chunks - 1, n, s_a, m_a, (m, acc))
            o = acc[:dh] / acc[dh:dh + 1]
            o_ref[hf * tq:(hf + 1) * tq, g * dh:(g + 1) * dh] = o.T.astype(o_ref.dtype)

    i = pl.program_id(1)

    @pl.when(i < ctx_steps)
    def _():
        run(-(-ctx_len // tk), True)

    @pl.when(i >= ctx_steps)
    def _():
        run(n_chunks, False)


def flash_attention(q_t, k, v_t, *, tq, ctx_len, halves=2):
    heads, dh, t = q_t.shape
    kv, n_chunks, rows, tk = v_t.shape
    group = heads // kv
    step = halves * tq
    pad = -ctx_len % step
    assert ctx_len % tq == 0 and (pad + t) % step == 0
    q_t = jnp.pad(q_t, ((0, 0), (0, 0), (pad, 0)))
    out = pl.pallas_call(
        functools.partial(_flash_body, tq=tq, tk=tk, group=group, halves=halves, ctx_len=ctx_len,
                          ctx_steps=(pad + ctx_len) // step, n_chunks=n_chunks),
        grid=(kv, (pad + t) // step),
        in_specs=[
            pl.BlockSpec((group, dh, step), lambda h, i: (h, 0, i)),
            pl.BlockSpec((None, t, dh), lambda h, i: (h, 0, 0)),
            pl.BlockSpec((None, n_chunks, rows, tk), lambda h, i: (h, 0, 0, 0)),
        ],
        out_specs=pl.BlockSpec((step, group * dh), lambda h, i: (i, h)),
        out_shape=jax.ShapeDtypeStruct((pad + t, heads * dh), BF16),
        scratch_shapes=[pltpu.VMEM((halves * group, tk, tq), F32), pltpu.VMEM((halves * group, tk, tq), F32),
                        pltpu.VMEM((halves * group, 1, tq), F32), pltpu.VMEM((halves * group, 1, tq), F32)],
        compiler_params=_params("parallel", "parallel"),
        name="flash_attention",
    )(q_t, k, v_t)
    return out[pad:]


def _convpool_body(ah_ref, ab_ref, ac_ref, pu_ref, ah_up, ac_up, pu_up, ah_dn, ac_dn, pu_dn,
                   cw_ref, pw_ref, ps_ref, yc_ref, yp_ref, xbuf, pbuf,
                   *, tr, ctx_tiles, n_tiles, ctx_len, seq_len):
    i = pl.program_id(0)
    has_up = jnp.logical_and(i != 0, i != ctx_tiles)
    has_dn = jnp.logical_and(i != ctx_tiles - 1, i != n_tiles - 1)

    f32 = lambda ref: ref[...].astype(F32)
    xbuf[0:HALO] = jnp.where(has_up, f32(ac_up) * f32(ah_up), 0.0)
    xbuf[HALO:HALO + tr] = f32(ac_ref) * f32(ah_ref)
    xbuf[HALO + tr:HALO + tr + HALO] = jnp.where(has_dn, f32(ac_dn) * f32(ah_dn), 0.0)
    conv = (cw_ref[0:1] * xbuf[HALO - 1:HALO - 1 + tr] + cw_ref[1:2] * xbuf[HALO:HALO + tr]
            + cw_ref[2:3] * xbuf[HALO + 1:HALO + 1 + tr])
    yc_ref[...] = (f32(ab_ref) * conv).astype(yc_ref.dtype)

    pbuf[0:HALO] = jnp.where(has_up, f32(pu_up), 0.0)
    pbuf[HALO:HALO + tr] = f32(pu_ref)
    pbuf[HALO + tr:HALO + tr + HALO] = jnp.where(has_dn, f32(pu_dn), 0.0)
    in_ctx = i < ctx_tiles
    pos = lax.broadcasted_iota(jnp.int32, (tr, 1), 0) + jnp.where(in_ctx, i, i - ctx_tiles) * tr
    length = jnp.where(in_ctx, ctx_len, seq_len)
    pg = pw_ref.shape[-1]
    for g, win in enumerate(POOL_WINDOWS):
        cols = slice(g * pg, (g + 1) * pg)
        total = None
        for s in range(-(win // 2), win - win // 2):
            piece = pbuf[HALO + s:HALO + s + tr, cols]
            total = piece if total is None else total + piece
        count = jnp.minimum(pos - win // 2 + win, length) - jnp.maximum(pos - win // 2, 0)
        pooled = total / count.astype(F32) - pbuf[HALO:HALO + tr, cols]
        y = jnp.dot(pooled.astype(BF16), pw_ref[g], preferred_element_type=F32) * ps_ref[:, cols]
        yp_ref[:, cols] = y.astype(yp_ref.dtype)


def conv_pool(p, conv_w, pool_w, pool_scale, layer, *, width, tr, ctx_len, seq_len):
    t = p.shape[0]
    n_tiles = t // tr
    ctx_tiles = ctx_len // tr
    assert ctx_len % tr == 0 and tr % HALO == 0 and max(POOL_WINDOWS) // 2 <= HALO
    hb = tr // HALO
    main = lambda c: pl.BlockSpec((tr, width), lambda i: (i, c))
    up = lambda c: pl.BlockSpec((HALO, width), lambda i: (jnp.maximum(i * hb - 1, 0), c))
    dn = lambda c: pl.BlockSpec((HALO, width), lambda i: (jnp.minimum((i + 1) * hb, t // HALO - 1), c))
    return pl.pallas_call(
        functools.partial(_convpool_body, tr=tr, ctx_tiles=ctx_tiles, n_tiles=n_tiles,
                          ctx_len=ctx_len, seq_len=seq_len),
        grid=(n_tiles,),
        in_specs=[main(0), main(1), main(2), main(4), up(0), up(2), up(4), dn(0), dn(2), dn(4),
                  pl.BlockSpec((None,) + conv_w.shape[1:], lambda i: (layer, 0, 0)),
                  pl.BlockSpec((None,) + pool_w.shape[1:], lambda i: (layer, 0, 0, 0)),
                  pl.BlockSpec((None, 1, width), lambda i: (layer, 0, 0))],
        out_specs=[pl.BlockSpec((tr, width), lambda i: (i, 0))] * 2,
        out_shape=[jax.ShapeDtypeStruct((t, width), BF16)] * 2,
        scratch_shapes=[pltpu.VMEM((tr + 2 * HALO, width), F32)] * 2,
        compiler_params=_params("parallel"),
        name="conv_pool",
    )(p, p, p, p, p, p, p, p, p, p, conv_w, pool_w, pool_scale.reshape(-1, 1, width))


def s5_chunk_weights(lam_re, lam_im, log_step, b_re, b_im, c_re, c_im):
    tc = S5_CHUNK
    lam = lax.complex(lam_re.astype(F32), lam_im.astype(F32))
    z = lam * jnp.exp(log_step.astype(F32))[..., None]
    lam_bar = jnp.exp(z)
    b_bar = ((lam_bar - 1.0) / lam)[..., None] * lax.complex(b_re.astype(F32), b_im.astype(F32))
    c = lax.complex(c_re.astype(F32), c_im.astype(F32))
    depth, _, groups, p_dim, s_dim = b_bar.shape
    powers = jnp.exp(z[..., None, :] * jnp.arange(tc + 1, dtype=F32)[:, None])

    cp = c[..., None, :, :] * powers[..., :tc, None, :]
    kern = (jnp.einsum('ldgtop,ldgpi->ldgtoi', jnp.real(cp), jnp.real(b_bar), precision=HIGHEST)
            - jnp.einsum('ldgtop,ldgpi->ldgtoi', jnp.imag(cp), jnp.imag(b_bar), precision=HIGHEST))
    k_f, k_b = kern[:, 0], kern[:, 1]
    by_lag = jnp.concatenate([k_b[:, :, :0:-1], k_f[:, :, :1] + k_b[:, :, :1], k_f[:, :, 1:],
                              jnp.zeros_like(k_f[:, :, :1])], axis=2)
    skew = jnp.tile(by_lag, (1, 1, tc, 1, 1))[:, :, :tc * (2 * tc - 1)]
    skew = skew.reshape(depth, groups, tc, 2 * tc - 1, s_dim, s_dim)[:, :, :, tc - 1:]
    toeplitz = jnp.transpose(skew, (0, 1, 2, 5, 3, 4)).reshape(depth, groups, tc * s_dim, tc * s_dim)

    f_f = c[:, 0][:, :, None] * powers[:, 0, :, 1:tc + 1, None, :]
    f_b = c[:, 1][:, :, None] * powers[:, 1, :, tc:0:-1, None, :]
    rows = lambda a: jnp.transpose(a, (0, 1, 4, 2, 3)).reshape(depth, groups, p_dim, tc * s_dim)
    w_out = jnp.concatenate([toeplitz, rows(jnp.real(f_f)), rows(jnp.real(f_b)),
                             rows(-jnp.imag(f_f)), rows(-jnp.imag(f_b))], axis=2)

    e_f = powers[:, 0, :, tc - 1::-1][:, :, :tc, :, None] * b_bar[:, 0][:, :, None]
    e_b = powers[:, 1, :, :tc, :, None] * b_bar[:, 1][:, :, None]
    cols = lambda a: jnp.transpose(a, (0, 1, 2, 4, 3)).reshape(depth, groups, tc * s_dim, p_dim)
    w_state = jnp.concatenate([cols(jnp.real(e_f)), cols(jnp.real(e_b)),
                               cols(jnp.imag(e_f)), cols(jnp.imag(e_b))], axis=3)

    last = powers[..., tc, :]
    decay = jnp.stack([jnp.concatenate([jnp.real(last[:, 0]), jnp.real(last[:, 1])], axis=-1),
                       jnp.concatenate([jnp.imag(last[:, 0]), jnp.imag(last[:, 1])], axis=-1)], axis=2)
    return _slab_operators(w_state, w_out, decay, tc, s_dim)


def _slab_operators(w_state, w_out, decay, tc, s_dim):
    depth, groups, cs, st = w_state.shape
    gl = LANES // s_dim
    k = groups // gl
    p_dim = st // 4
    w_state, w_out = lax.optimization_barrier((w_state.astype(BF16), w_out.astype(BF16)))

    def stacked_rows(a, piece):
        rows, cols = a.shape[2:]
        a = a.reshape(depth, k, gl, rows // piece, piece, cols)
        return jnp.transpose(a, (0, 1, 3, 2, 4, 5)).reshape(depth, k, gl * rows, cols).astype(BF16)

    def spread(cols, piece):
        c = np.arange(gl * cols)
        src = c // (gl * piece) * piece + c % piece
        return jnp.asarray(np.arange(cols)[:, None] == src[None, :], dtype=BF16)

    ws = expand_block_diagonal(stacked_rows(w_state, s_dim), spread(st, p_dim), s_dim, p_dim, gl)
    top = expand_block_diagonal(stacked_rows(w_out[:, :, :cs], s_dim), spread(cs, s_dim), s_dim, s_dim, gl)
    bot = expand_block_diagonal(stacked_rows(w_out[:, :, cs:], p_dim), spread(cs, s_dim), p_dim, s_dim, gl)
    decay_s = jnp.transpose(decay.reshape(depth, k, gl, 2, 2, p_dim), (0, 1, 3, 4, 2, 5))
    return ws, top, bot, decay_s.reshape(depth, k, 2, 2 * gl * p_dim)


def _expand_body(a_ref, c_ref, o_ref, *, row_piece, col_piece, gl):
    r = jnp.dot(a_ref[...], c_ref[...], preferred_element_type=F32)
    row0 = pl.program_id(2) * a_ref.shape[0]
    row_group = (row0 + lax.broadcasted_iota(jnp.int32, (a_ref.shape[0], 1), 0)) // row_piece % gl
    col_group = lax.broadcasted_iota(jnp.int32, (1, c_ref.shape[1]), 1) // col_piece % gl
    o_ref[...] = jnp.where(row_group == col_group, r, 0.0).astype(o_ref.dtype)


def expand_block_diagonal(a, spread, row_piece, col_piece, gl):
    depth, k, rows, cols = a.shape
    tr = _tile(rows, 512, 16)
    return pl.pallas_call(
        functools.partial(_expand_body, row_piece=row_piece, col_piece=col_piece, gl=gl),
        grid=(depth, k, rows // tr),
        in_specs=[pl.BlockSpec((None, None, tr, cols), lambda l, i, r: (l, i, r, 0)),
                  pl.BlockSpec(spread.shape, lambda l, i, r: (0, 0))],
        out_specs=pl.BlockSpec((None, None, tr, gl * cols), lambda l, i, r: (l, i, r, 0)),
        out_shape=jax.ShapeDtypeStruct((depth, k, rows, gl * cols), BF16),
        compiler_params=_params("parallel", "parallel", "parallel"),
        name="expand_block_diagonal",
    )(a, spread)


def _s5_state_body(u_ref, w_ref, decay_ref, h_ref, s_scr, *, n, ctx_chunks):
    half = decay_ref.shape[-1]
    q = half // 2
    rows = n // 2 if n % 16 == 0 else n
    for r in range(0, n, rows):
        s_scr[r:r + rows] = jnp.dot(u_ref[r:r + rows], w_ref[...], preferred_element_type=F32)
    parts = ((slice(0, q), slice(half, half + q)), (slice(q, half), slice(half + q, 2 * half)))
    decays = [(decay_ref[0:1, lo:lo + q], decay_ref[1:2, lo:lo + q]) for lo in (0, q)]

    def scan(j, carry):
        jb = jnp.where(j < ctx_chunks, ctx_chunks - 1 - j, n - 1 + ctx_chunks - j)
        out = []
        for row, (re, im), (a_re, a_im), (c_re, c_im) in zip((j, jb), parts, decays, carry):
            s_re, s_im = s_scr[pl.ds(row, 1), re], s_scr[pl.ds(row, 1), im]
            s_scr[pl.ds(row, 1), re] = c_re
            s_scr[pl.ds(row, 1), im] = c_im
            out.append((a_re * c_re - a_im * c_im + s_re, a_re * c_im + a_im * c_re + s_im))
        return tuple(out)

    zero = jnp.zeros((1, q), F32)
    lax.fori_loop(0, n, scan, ((zero, zero), (zero, zero)))
    h_ref[...] = s_scr[...].astype(h_ref.dtype)


def s5_state(u, w_state_s, decay_s, layer, *, ctx_chunks):
    k, n, cs = u.shape
    st = w_state_s.shape[-1]
    return pl.pallas_call(
        functools.partial(_s5_state_body, n=n, ctx_chunks=ctx_chunks),
        grid=(k,),
        in_specs=[
            pl.BlockSpec((None, n, cs), lambda i: (i, 0, 0)),
            pl.BlockSpec((None, None, cs, st), lambda i: (layer, i, 0, 0)),
            pl.BlockSpec((None, None, 2, st // 2), lambda i: (layer, i, 0, 0)),
        ],
        out_specs=pl.BlockSpec((None, n, st), lambda i: (i, 0, 0)),
        out_shape=jax.ShapeDtypeStruct((k, n, st), BF16),
        scratch_shapes=[pltpu.VMEM((n, st), F32)],
        compiler_params=_params("parallel"),
        name="s5_state",
    )(u, w_state_s, decay_s)


def _s5_readout_body(u_ref, h_ref, wu_ref, wh_ref, y_ref):
    y_ref[...] = (jnp.dot(u_ref[...], wu_ref[...], preferred_element_type=F32)
                  + jnp.dot(h_ref[...], wh_ref[...], preferred_element_type=F32))


def s5_readout(u, h, w_in_chunk, w_from_state, layer, *, tn):
    k, n, cs = u.shape
    st = h.shape[-1]
    return pl.pallas_call(
        _s5_readout_body,
        grid=(k, cs // tn),
        in_specs=[
            pl.BlockSpec((None, n, cs), lambda i, j: (i, 0, 0)),
            pl.BlockSpec((None, n, st), lambda i, j: (i, 0, 0)),
            pl.BlockSpec((None, None, cs, tn), lambda i, j: (layer, i, 0, j)),
            pl.BlockSpec((None, None, st, tn), lambda i, j: (layer, i, 0, j)),
        ],
        out_specs=pl.BlockSpec((None, n, tn), lambda i, j: (i, 0, j)),
        out_shape=jax.ShapeDtypeStruct((k, n, cs), F32),
        compiler_params=_params("parallel", "parallel"),
        name="s5_readout",
    )(u, h, w_in_chunk, w_from_state)


def _s5_glu_body(y_ref, u_ref, d_ref, w_ref, o_ref):
    z = jax.nn.gelu(y_ref[...] + d_ref[...] * u_ref[...].astype(F32))
    gate = _sigmoid(jnp.dot(z.astype(BF16), w_ref[...], preferred_element_type=F32))
    o_ref[...] = (z * gate).astype(o_ref.dtype)


def s5_glu(y, p, d, w_glu, layer, *, width, tr):
    t = p.shape[0]
    return pl.pallas_call(
        _s5_glu_body,
        grid=(t // tr,),
        in_specs=[
            pl.BlockSpec((tr, width), lambda i: (i, 0)),
            pl.BlockSpec((tr, width), lambda i: (i, 3)),
            pl.BlockSpec((None, 1, width), lambda i: (layer, 0, 0)),
            pl.BlockSpec((None, width, width), lambda i: (layer, 0, 0)),
        ],
        out_specs=pl.BlockSpec((tr, width), lambda i: (i, 0)),
        out_shape=jax.ShapeDtypeStruct((t, width), BF16),
        compiler_params=_params("parallel"),
        name="s5_glu",
    )(y, p, d.reshape(-1, 1, width), w_glu)


def _merge_body(gl_ref, yc_ref, ys_ref, yp_ref, ya_ref, gu_ref, gb_ref, wc_ref, ws_ref, wp_ref, wa_ref,
                o_ref):
    g_low = gl_ref[...]
    total = None
    for b, (y_ref, w_ref) in enumerate(((yc_ref, wc_ref), (ys_ref, ws_ref), (yp_ref, wp_ref),
                                        (ya_ref, wa_ref))):
        gate = _sigmoid(jnp.dot(g_low, gu_ref[b], preferred_element_type=F32) + gb_ref[b])
        term = gate * jnp.dot(y_ref[...], w_ref[...], preferred_element_type=F32)
        total = term if total is None else total + term
    o_ref[...] = total.astype(o_ref.dtype)


def merge(p, ys, gate_up, gate_bias, w_outs, layer, *, gate_col, tm, tn):
    t = p.shape[0]
    _, nb, rank, d = gate_up.shape
    row = lambda a: pl.BlockSpec((tm, a.shape[1]), lambda i, j: (i, 0))
    col = lambda a: pl.BlockSpec((None, a.shape[1], tn), lambda i, j: (layer, 0, j))
    return pl.pallas_call(
        _merge_body,
        grid=(t // tm, d // tn),
        in_specs=[pl.BlockSpec((tm, rank), lambda i, j: (i, gate_col))] + [row(y) for y in ys]
                 + [pl.BlockSpec((None, nb, rank, tn), lambda i, j: (layer, 0, 0, j)),
                    pl.BlockSpec((None, nb, 1, tn), lambda i, j: (layer, 0, 0, j))] + [col(w) for w in w_outs],
        out_specs=pl.BlockSpec((tm, tn), lambda i, j: (i, j)),
        out_shape=jax.ShapeDtypeStruct((t, d), BF16),
        compiler_params=_params("parallel", "parallel"),
        name="merge",
    )(p, *ys, gate_up, gate_bias.reshape(-1, nb, 1, d), *w_outs)


def _rope_tables(ctx_len, seq_len, dh):
    pairs = dh // 4
    rows = seq_len // GRID_W
    row = jnp.repeat(jnp.arange(rows), GRID_W).astype(F32)
    col = jnp.tile(jnp.arange(GRID_W), rows).astype(F32)
    inv = ROPE_THETA ** (-jnp.arange(pairs, dtype=F32) / pairs)
    ang = jnp.concatenate([row[:, None] * inv, col[:, None] * inv], axis=-1)
    cos = jnp.repeat(jnp.cos(ang), 2, axis=-1)
    sin = jnp.stack([-jnp.sin(ang), jnp.sin(ang)], axis=-1).reshape(seq_len, dh)
    cos = jnp.concatenate([jnp.ones((ctx_len, dh), F32), cos], axis=0)
    sin = jnp.concatenate([jnp.zeros((ctx_len, dh), F32), sin], axis=0)
    return cos, sin


def kernel(x, c, ctx, c_ctx, ada_down, ada_up, ada_bias, norm_gains, w_in, conv_w, s5_lambda_re, s5_lambda_im, s5_log_step, s5_b_re, s5_b_im, s5_c_re, s5_c_im, s5_d, s5_w_glu, pool_w, pool_scale, qk_norm, gate_up, gate_bias, w_out_conv, w_out_s5, w_out_pool, w_out_attn, w_o, ffn_w_gate, ffn_w_up, ffn_w_down):
    batch, seq_len, d = x.shape
    ctx_len = ctx.shape[1]
    depth = w_in.shape[0]
    assert batch == 1 and c.shape[0] == 1
    t = ctx_len + seq_len
    width = conv_w.shape[-1]
    assert s5_d.shape[-1] == width and pool_scale.shape[-1] == width
    dh = qk_norm.shape[-1]
    q_width = w_out_attn.shape[1]
    rank = gate_up.shape[2]
    in_width = w_in.shape[-1]
    kv_width = (in_width - 5 * width - q_width - rank) // 2
    heads, kv_heads = q_width // dh, kv_width // dh
    groups, s_dim = s5_b_re.shape[2], s5_b_re.shape[-1]
    slabs = width // LANES
    assert groups * s_dim == width and LANES % s_dim == 0
    hidden = ffn_w_gate.shape[-1]
    q_col, k_col, v_col = 5 * width, 5 * width + q_width, 5 * width + q_width + kv_width
    gate_col = (v_col + kv_width) // rank
    assert q_col % dh == 0 and (v_col + kv_width) % rank == 0 and width % dh == 0

    tr = _tile(ctx_len, 256, 8)
    tm = _tile(t, 1280, 256)
    tm_half = _tile(t, 640, 128)
    tk_att = _tile(t, 1280, 256)
    tq = _tile(ctx_len, 256, 128)
    ctx_tiles = ctx_len // tr

    w_in_b, w_o_b, w_gate_b, w_up_b, w_down_b = (w.astype(BF16) for w in (w_in, w_o, ffn_w_gate, ffn_w_up,
                                                                         ffn_w_down))
    w_outs_b = [w.astype(BF16) for w in (w_out_conv, w_out_s5, w_out_pool, w_out_attn)]
    gate_up_b, pool_w_b, w_glu_b = gate_up.astype(BF16), pool_w.astype(BF16), s5_w_glu.astype(BF16)
    w_state, w_s5top, w_s5bot, decay = s5_chunk_weights(s5_lambda_re, s5_lambda_im, s5_log_step,
                                               s5_b_re, s5_b_im, s5_c_re, s5_c_im)
    cos, sin = _rope_tables(ctx_len, seq_len, dh)
    qk_gains = qk_norm.reshape(depth, 2, 1, dh)

    cond = jnp.concatenate([c_ctx[None, :], c, jnp.zeros((6, d), F32)], axis=0)
    mods = ada_modulation(cond, ada_down, ada_up, ada_bias)
    mods = mods[:, :2].reshape(depth, 2, N_MOD, d)
    mods = jnp.concatenate([mods, jnp.zeros((depth, 2, 8 - N_MOD, d), F32)], axis=2)

    stream = jnp.concatenate([ctx[0], x[0]], axis=0)
    row_kw = dict(ctx_tiles=ctx_tiles, tr=tr)
    h = modulate(stream, mods, norm_gains, 0, rows=(0, 1, 0), **row_kw)
    n_chunks = t // S5_CHUNK
    for l in range(depth):
        p = matmul(h, w_in_b, l, tm=tm, tn=_tile(in_width, 1024, 256), out_dtype=BF16)

        q_t = qk_prep(p, qk_gains, cos, sin, l, 0, col0=q_col // dh, heads=heads, tr=tk_att,
                      scale=dh ** -0.5 * math.log2(math.e), transpose=True)
        k_n = qk_prep(p, qk_gains, cos, sin, l, 1, col0=k_col // dh, heads=kv_heads, tr=tk_att,
                      scale=1.0, transpose=False)
        v_t = v_prep(p, col0=v_col // dh, heads=kv_heads, dh=dh, tk=tk_att)
        y_attn = flash_attention(q_t, k_n, v_t, tq=tq, ctx_len=ctx_len)

        y_conv, y_pool = conv_pool(p, conv_w, pool_w_b, pool_scale, l, width=width, tr=tr,
                                   ctx_len=ctx_len, seq_len=seq_len)

        u = p[:, 3 * width:4 * width].reshape(n_chunks, S5_CHUNK, slabs, LANES)
        u = jnp.transpose(u, (2, 0, 1, 3)).reshape(slabs, n_chunks, S5_CHUNK * LANES)
        h_s5 = s5_state(u, w_state, decay, l, ctx_chunks=ctx_len // S5_CHUNK)
        y_s5 = s5_readout(u, h_s5, w_s5top, w_s5bot, l, tn=_tile(S5_CHUNK * LANES, 1024, 256))
        y_s5 = jnp.transpose(y_s5.reshape(slabs, n_chunks, S5_CHUNK, LANES), (1, 2, 0, 3)).reshape(t, width)
        y_s5 = s5_glu(y_s5, p, s5_d, w_glu_b, l, width=width, tr=tm_half)

        merged = merge(p, [y_conv, y_s5, y_pool, y_attn], gate_up_b, gate_bias, w_outs_b, l,
                       gate_col=gate_col, tm=tm, tn=_tile(d, 256, 256))
        mix = matmul(merged, w_o_b, l, tm=tm, tn=_tile(d, 1024, 256), out_dtype=BF16)
        stream, h = residual(stream, mix, mods, norm_gains, l, gate_row=2, post_row=1, nxt=(3, 4, 2), **row_kw)

        hid = ffn_up(h, w_gate_b, w_up_b, l, tm=_tile(t, 1664, 128), tn=_tile(hidden, 256, 128))
        ffn = matmul(hid, w_down_b, l, tm=_tile(t, 832, 64), tn=_tile(d, 256, 256), out_dtype=BF16)
        if l + 1 < depth:
            stream, h = residual(stream, ffn, mods, norm_gains, l, gate_row=5, post_row=3, nxt=(0, 1, 0),
                                 next_layer=l + 1, **row_kw)
        else:
            stream, _ = residual(stream, ffn, mods, norm_gains, l, gate_row=5, post_row=3,
                                 first_tile=ctx_tiles, **row_kw)
    return stream[None]
```

```python
import functools
import math

import jax
import jax.numpy as jnp
import numpy as np
from jax import lax
from jax.experimental import pallas as pl
from jax.experimental.pallas import tpu as pltpu

EPS = 1e-6
GRID_W = 64
ROPE_THETA = 10000.0
POOL_WINDOWS = (2, 4, 8, 16)
N_MOD = 6
S5_CHUNK = 16
HALO = 16
LANES = 128
ONES_ROWS = 16
NEG_BIG = -1e30
V7X_VMEM_LIMIT = 56 * 1024 * 1024
F32 = jnp.float32
BF16 = jnp.bfloat16
HIGHEST = lax.Precision.HIGHEST


def _params(*sem):
    return pltpu.CompilerParams(dimension_semantics=sem, vmem_limit_bytes=V7X_VMEM_LIMIT)


def _tile(n, target, mult):
    best = None
    for t in range(mult, min(n, target) + 1, mult):
        if n % t == 0:
            best = t
    assert best is not None, (n, target, mult)
    return best


def _rms(x, gain):
    return x * lax.rsqrt(jnp.mean(x * x, axis=-1, keepdims=True) + EPS) * gain


def _sigmoid(x):
    return 1.0 / (1.0 + jnp.exp(-x))


def _ada_body(cond_ref, down_ref, up_ref, bias_ref, o_ref):
    s = cond_ref[...]
    s = s * _sigmoid(s)
    low = jnp.dot(s, down_ref[...], precision=HIGHEST, preferred_element_type=F32)
    o_ref[...] = jnp.dot(low, up_ref[...], precision=HIGHEST, preferred_element_type=F32) + bias_ref[...]


def ada_modulation(cond, down, up, bias):
    depth, d, rank = down.shape
    width = up.shape[-1]
    tn = _tile(width, 4096, 128)
    return pl.pallas_call(
        _ada_body,
        grid=(depth, width // tn),
        in_specs=[
            pl.BlockSpec((8, d), lambda l, j: (0, 0)),
            pl.BlockSpec((None, d, rank), lambda l, j: (l, 0, 0)),
            pl.BlockSpec((None, rank, tn), lambda l, j: (l, 0, j)),
            pl.BlockSpec((None, 1, tn), lambda l, j: (l, 0, j)),
        ],
        out_specs=pl.BlockSpec((None, 8, tn), lambda l, j: (l, 0, j)),
        out_shape=jax.ShapeDtypeStruct((depth, 8, width), F32),
        compiler_params=_params("arbitrary", "arbitrary"),
        name="ada_modulation",
    )(cond, down, up, bias.reshape(depth, 1, width))


def _modulated(x, mod, gains, rows):
    shift_row, scale_row, gain_row = rows
    h = _rms(x, gains[gain_row:gain_row + 1])
    return h * (1.0 + mod[scale_row:scale_row + 1]) + mod[shift_row:shift_row + 1]


def _modulate_body(x_ref, mod_ref, gain_ref, h_ref, *, rows):
    h_ref[...] = _modulated(x_ref[...], mod_ref[...], gain_ref[...], rows).astype(h_ref.dtype)


def modulate(x, mods, gains, layer, *, ctx_tiles, tr, rows):
    t, d = x.shape
    return pl.pallas_call(
        functools.partial(_modulate_body, rows=rows),
        grid=(t // tr,),
        in_specs=[
            pl.BlockSpec((tr, d), lambda i: (i, 0)),
            pl.BlockSpec((None, None, 8, d), lambda i: (layer, jnp.where(i < ctx_tiles, 0, 1), 0, 0)),
            pl.BlockSpec((None,) + gains.shape[1:], lambda i: (layer, 0, 0)),
        ],
        out_specs=pl.BlockSpec((tr, d), lambda i: (i, 0)),
        out_shape=jax.ShapeDtypeStruct((t, d), BF16),
        compiler_params=_params("parallel"),
        name="modulate",
    )(x, mods, gains)


def _residual_body(x_ref, y_ref, mod_ref, gain_ref, modn_ref, gainn_ref, xo_ref, *h_ref,
                   gate_row, post_row, nxt):
    mod = mod_ref[...]
    xn = x_ref[...] + mod[gate_row:gate_row + 1] * _rms(y_ref[...].astype(F32), gain_ref[post_row:post_row + 1])
    xo_ref[...] = xn
    if nxt is not None:
        h_ref[0][...] = _modulated(xn, modn_ref[...], gainn_ref[...], nxt).astype(h_ref[0].dtype)


def residual(x, y, mods, gains, layer, *, ctx_tiles, tr, gate_row, post_row, nxt=None, next_layer=None,
             first_tile=0):
    t, d = x.shape
    n_tiles = t // tr - first_tile
    if next_layer is None:
        next_layer = layer
    who = lambda i: jnp.where(i + first_tile < ctx_tiles, 0, 1)
    row_in = pl.BlockSpec((tr, d), lambda i: (i + first_tile, 0))
    row_out = pl.BlockSpec((tr, d), lambda i: (i, 0))
    mod_spec = lambda l: pl.BlockSpec((None, None, 8, d), lambda i: (l, who(i), 0, 0))
    gain_spec = lambda l: pl.BlockSpec((None,) + gains.shape[1:], lambda i: (l, 0, 0))
    out_specs = [row_out]
    out_shape = [jax.ShapeDtypeStruct((n_tiles * tr, d), F32)]
    if nxt is not None:
        out_specs.append(row_out)
        out_shape.append(jax.ShapeDtypeStruct((n_tiles * tr, d), BF16))
    out = pl.pallas_call(
        functools.partial(_residual_body, gate_row=gate_row, post_row=post_row, nxt=nxt),
        grid=(n_tiles,),
        in_specs=[row_in, row_in, mod_spec(layer), gain_spec(layer), mod_spec(next_layer),
                  gain_spec(next_layer)],
        out_specs=out_specs,
        out_shape=out_shape,
        compiler_params=_params("parallel"),
        name="residual_norm",
    )(x, y, mods, gains, mods, gains)
    return out if nxt is not None else (out[0], None)


def _matmul_body(a_ref, b_ref, o_ref):
    o_ref[...] = jnp.dot(a_ref[...], b_ref[...], preferred_element_type=F32).astype(o_ref.dtype)


def matmul(a, b, layer, *, tm, tn, out_dtype=BF16):
    m, k = a.shape
    n = b.shape[-1]
    return pl.pallas_call(
        _matmul_body,
        grid=(m // tm, n // tn),
        in_specs=[
            pl.BlockSpec((tm, k), lambda i, j: (i, 0)),
            pl.BlockSpec((None, k, tn), lambda i, j: (layer, 0, j)),
        ],
        out_specs=pl.BlockSpec((tm, tn), lambda i, j: (i, j)),
        out_shape=jax.ShapeDtypeStruct((m, n), out_dtype),
        compiler_params=_params("parallel", "parallel"),
        name="matmul",
    )(a, b)


def _ffn_up_body(a_ref, wg_ref, wu_ref, o_ref):
    a = a_ref[...]
    g = jnp.dot(a, wg_ref[...], preferred_element_type=F32)
    u = jnp.dot(a, wu_ref[...], preferred_element_type=F32)
    o_ref[...] = (g * _sigmoid(g) * u).astype(o_ref.dtype)


def ffn_up(a, w_gate, w_up, layer, *, tm, tn):
    m, k = a.shape
    n = w_gate.shape[-1]
    w_spec = pl.BlockSpec((None, k, tn), lambda i, j: (layer, 0, j))
    return pl.pallas_call(
        _ffn_up_body,
        grid=(m // tm, n // tn),
        in_specs=[pl.BlockSpec((tm, k), lambda i, j: (i, 0)), w_spec, w_spec],
        out_specs=pl.BlockSpec((tm, tn), lambda i, j: (i, j)),
        out_shape=jax.ShapeDtypeStruct((m, n), BF16),
        compiler_params=_params("parallel", "parallel"),
        name="ffn_up",
    )(a, w_gate, w_up)


def _qk_prep_body(x_ref, gain_ref, cos_ref, sin_ref, o_ref, *, scale, transpose):
    y = _rms(x_ref[...].astype(F32), gain_ref[...])
    lanes = y.shape[-1]
    even = lax.broadcasted_iota(jnp.int32, (1, lanes), 1) % 2 == 0
    partner = jnp.where(even, pltpu.roll(y, lanes - 1, 1), pltpu.roll(y, 1, 1))
    y = (y * cos_ref[...] + partner * sin_ref[...]) * scale
    if transpose:
        o_ref[...] = y.T.astype(o_ref.dtype)
    else:
        o_ref[...] = y.astype(o_ref.dtype)


def qk_prep(p, gains, cos, sin, layer, which, *, col0, heads, tr, scale, transpose):
    t = p.shape[0]
    dh = gains.shape[-1]
    if transpose:
        out_spec = pl.BlockSpec((None, dh, tr), lambda i, h: (h, 0, i))
        out_shape = jax.ShapeDtypeStruct((heads, dh, t), BF16)
    else:
        out_spec = pl.BlockSpec((None, tr, dh), lambda i, h: (h, i, 0))
        out_shape = jax.ShapeDtypeStruct((heads, t, dh), BF16)
    return pl.pallas_call(
        functools.partial(_qk_prep_body, scale=scale, transpose=transpose),
        grid=(t // tr, heads),
        in_specs=[
            pl.BlockSpec((tr, dh), lambda i, h: (i, col0 + h)),
            pl.BlockSpec((None, None, 1, dh), lambda i, h: (layer, which, 0, 0)),
            pl.BlockSpec((tr, dh), lambda i, h: (i, 0)),
            pl.BlockSpec((tr, dh), lambda i, h: (i, 0)),
        ],
        out_specs=out_spec,
        out_shape=out_shape,
        compiler_params=_params("parallel", "parallel"),
        name="qk_prep",
    )(p, gains, cos, sin)


def _v_prep_body(x_ref, o_ref):
    dh = x_ref.shape[1]
    o_ref[:dh] = x_ref[...].astype(F32).T.astype(o_ref.dtype)
    first = lax.broadcasted_iota(jnp.int32, (ONES_ROWS, o_ref.shape[1]), 0) == 0
    o_ref[dh:] = jnp.where(first, 1.0, 0.0).astype(o_ref.dtype)


def v_prep(p, *, col0, heads, dh, tk):
    t = p.shape[0]
    return pl.pallas_call(
        _v_prep_body,
        grid=(t // tk, heads),
        in_specs=[pl.BlockSpec((tk, dh), lambda i, h: (i, col0 + h))],
        out_specs=pl.BlockSpec((None, None, dh + ONES_ROWS, tk), lambda i, h: (h, i, 0, 0)),
        out_shape=jax.ShapeDtypeStruct((heads, t // tk, dh + ONES_ROWS, tk), BF16),
        compiler_params=_params("parallel", "parallel"),
        name="v_prep",
    )(p)


def _flash_body(q_ref, k_ref, v_ref, o_ref, s_a, s_b, m_a, m_b,
                *, tq, tk, group, halves, ctx_len, ctx_steps, n_chunks):
    dh = q_ref.shape[1]
    rows = v_ref.shape[1]
    streams = [(g, hf) for hf in range(halves) for g in range(group)]

    def scores(c, n, s_ref, m_ref, masked):
        g, hf = streams[n]
        k_c = k_ref[pl.ds(pl.multiple_of(c * tk, tk), tk), :]
        s = jnp.dot(k_c, q_ref[g, :, hf * tq:(hf + 1) * tq], preferred_element_type=F32)
        if masked:
            key = c * tk + lax.broadcasted_iota(jnp.int32, (tk, 1), 0)
            s = jnp.where(key < ctx_len, s, NEG_BIG)
        s_ref[n] = s
        m_ref[n] = jnp.max(s, axis=0, keepdims=True)

    def absorb(c, n, s_ref, m_ref, carry):
        m, acc = carry
        m_new = jnp.maximum(m, m_ref[n])
        p = jnp.exp2(s_ref[n] - m_new).astype(BF16)
        acc = jnp.exp2(m - m_new) * acc + jnp.dot(v_ref[c], p, preferred_element_type=F32)
        return m_new, acc

    def run(chunks, masked):
        ns = len(streams)
        carry = tuple((jnp.full((1, tq), NEG_BIG, F32), jnp.zeros((rows, tq), F32)) for _ in range(ns))
        for n in range(ns):
            scores(0, n, s_a, m_a, masked)

        def half(c_next, c_cur, nxt, cur, carry):
            out = []
            for n in range(ns):
                scores(c_next, n, *nxt, masked)
                out.append(absorb(c_cur, n, *cur, carry[n]))
            return tuple(out)

        def pair(c, carry):
            carry = half(c + 1, c, (s_b, m_b), (s_a, m_a), carry)
            return half(jnp.minimum(c + 2, chunks - 1), c + 1, (s_a, m_a), (s_b, m_b), carry)

        def quad(i, carry):
            return pair(4 * i + 2, pair(4 * i, carry))

        for i in range(chunks // 4):
            carry = quad(i, carry)
        if chunks % 4 >= 2:
            carry = pair(chunks // 4 * 4, carry)
        for n, (g, hf) in enumerate(streams):
            m, acc = carry[n]
            if chunks % 2 == 1:
                m, acc = absorb(chunks - 1, n, s_a, m_a, (m, acc))
            o = acc[:dh] / acc[dh:dh + 1]
            o_ref[hf * tq:(hf + 1) * tq, g * dh:(g + 1) * dh] = o.T.astype(o_ref.dtype)

    i = pl.program_id(1)

    @pl.when(i < ctx_steps)
    def _():
        run(-(-ctx_len // tk), True)

    @pl.when(i >= ctx_steps)
    def _():
        run(n_chunks, False)


def flash_attention(q_t, k, v_t, *, tq, ctx_len, halves=2):
    heads, dh, t = q_t.shape
    kv, n_chunks, rows, tk = v_t.shape
    group = heads // kv
    step = halves * tq
    pad = -ctx_len % step
    assert ctx_len % tq == 0 and (pad + t) % step == 0
    q_t = jnp.pad(q_t, ((0, 0), (0, 0), (pad, 0)))
    out = pl.pallas_call(
        functools.partial(_flash_body, tq=tq, tk=tk, group=group, halves=halves, ctx_len=ctx_len,
                          ctx_steps=(pad + ctx_len) // step, n_chunks=n_chunks),
        grid=(kv, (pad + t) // step),
        in_specs=[
            pl.BlockSpec((group, dh, step), lambda h, i: (h, 0, i)),
            pl.BlockSpec((None, t, dh), lambda h, i: (h, 0, 0)),
            pl.BlockSpec((None, n_chunks, rows, tk), lambda h, i: (h, 0, 0, 0)),
        ],
        out_specs=pl.BlockSpec((step, group * dh), lambda h, i: (i, h)),
        out_shape=jax.ShapeDtypeStruct((pad + t, heads * dh), BF16),
        scratch_shapes=[pltpu.VMEM((halves * group, tk, tq), F32), pltpu.VMEM((halves * group, tk, tq), F32),
                        pltpu.VMEM((halves * group, 1, tq), F32), pltpu.VMEM((halves * group, 1, tq), F32)],
        compiler_params=_params("parallel", "parallel"),
        name="flash_attention",
    )(q_t, k, v_t)
    return out[pad:]


def _convpool_body(ah_ref, ab_ref, ac_ref, pu_ref, ah_up, ac_up, pu_up, ah_dn, ac_dn, pu_dn,
                   cw_ref, pw_ref, ps_ref, yc_ref, yp_ref, xbuf, pbuf,
                   *, tr, ctx_tiles, n_tiles, ctx_len, seq_len):
    i = pl.program_id(0)
    has_up = jnp.logical_and(i != 0, i != ctx_tiles)
    has_dn = jnp.logical_and(i != ctx_tiles - 1, i != n_tiles - 1)

    f32 = lambda ref: ref[...].astype(F32)
    xbuf[0:HALO] = jnp.where(has_up, f32(ac_up) * f32(ah_up), 0.0)
    xbuf[HALO:HALO + tr] = f32(ac_ref) * f32(ah_ref)
    xbuf[HALO + tr:HALO + tr + HALO] = jnp.where(has_dn, f32(ac_dn) * f32(ah_dn), 0.0)
    conv = (cw_ref[0:1] * xbuf[HALO - 1:HALO - 1 + tr] + cw_ref[1:2] * xbuf[HALO:HALO + tr]
            + cw_ref[2:3] * xbuf[HALO + 1:HALO + 1 + tr])
    yc_ref[...] = (f32(ab_ref) * conv).astype(yc_ref.dtype)

    pbuf[0:HALO] = jnp.where(has_up, f32(pu_up), 0.0)
    pbuf[HALO:HALO + tr] = f32(pu_ref)
    pbuf[HALO + tr:HALO + tr + HALO] = jnp.where(has_dn, f32(pu_dn), 0.0)
    in_ctx = i < ctx_tiles
    pos = lax.broadcasted_iota(jnp.int32, (tr, 1), 0) + jnp.where(in_ctx, i, i - ctx_tiles) * tr
    length = jnp.where(in_ctx, ctx_len, seq_len)
    pg = pw_ref.shape[-1]
    for g, win in enumerate(POOL_WINDOWS):
        cols = slice(g * pg, (g + 1) * pg)
        total = None
        for s in range(-(win // 2), win - win // 2):
            piece = pbuf[HALO + s:HALO + s + tr, cols]
            total = piece if total is None else total + piece
        count = jnp.minimum(pos - win // 2 + win, length) - jnp.maximum(pos - win // 2, 0)
        pooled = total / count.astype(F32) - pbuf[HALO:HALO + tr, cols]
        y = jnp.dot(pooled.astype(BF16), pw_ref[g], preferred_element_type=F32) * ps_ref[:, cols]
        yp_ref[:, cols] = y.astype(yp_ref.dtype)


def conv_pool(p, conv_w, pool_w, pool_scale, layer, *, width, tr, ctx_len, seq_len):
    t = p.shape[0]
    n_tiles = t // tr
    ctx_tiles = ctx_len // tr
    assert ctx_len % tr == 0 and tr % HALO == 0 and max(POOL_WINDOWS) // 2 <= HALO
    hb = tr // HALO
    main = lambda c: pl.BlockSpec((tr, width), lambda i: (i, c))
    up = lambda c: pl.BlockSpec((HALO, width), lambda i: (jnp.maximum(i * hb - 1, 0), c))
    dn = lambda c: pl.BlockSpec((HALO, width), lambda i: (jnp.minimum((i + 1) * hb, t // HALO - 1), c))
    return pl.pallas_call(
        functools.partial(_convpool_body, tr=tr, ctx_tiles=ctx_tiles, n_tiles=n_tiles,
                          ctx_len=ctx_len, seq_len=seq_len),
        grid=(n_tiles,),
        in_specs=[main(0), main(1), main(2), main(4), up(0), up(2), up(4), dn(0), dn(2), dn(4),
                  pl.BlockSpec((None,) + conv_w.shape[1:], lambda i: (layer, 0, 0)),
                  pl.BlockSpec((None,) + pool_w.shape[1:], lambda i: (layer, 0, 0, 0)),
                  pl.BlockSpec((None, 1, width), lambda i: (layer, 0, 0))],
        out_specs=[pl.BlockSpec((tr, width), lambda i: (i, 0))] * 2,
        out_shape=[jax.ShapeDtypeStruct((t, width), BF16)] * 2,
        scratch_shapes=[pltpu.VMEM((tr + 2 * HALO, width), F32)] * 2,
        compiler_params=_params("parallel"),
        name="conv_pool",
    )(p, p, p, p, p, p, p, p, p, p, conv_w, pool_w, pool_scale.reshape(-1, 1, width))


def s5_chunk_weights(lam_re, lam_im, log_step, b_re, b_im, c_re, c_im):
    tc = S5_CHUNK
    lam = lax.complex(lam_re.astype(F32), lam_im.astype(F32))
    z = lam * jnp.exp(log_step.astype(F32))[..., None]
    lam_bar = jnp.exp(z)
    b_bar = ((lam_bar - 1.0) / lam)[..., None] * lax.complex(b_re.astype(F32), b_im.astype(F32))
    c = lax.complex(c_re.astype(F32), c_im.astype(F32))
    depth, _, groups, p_dim, s_dim = b_bar.shape
    powers = jnp.exp(z[..., None, :] * jnp.arange(tc + 1, dtype=F32)[:, None])

    cp = c[..., None, :, :] * powers[..., :tc, None, :]
    kern = (jnp.einsum('ldgtop,ldgpi->ldgtoi', jnp.real(cp), jnp.real(b_bar), precision=HIGHEST)
            - jnp.einsum('ldgtop,ldgpi->ldgtoi', jnp.imag(cp), jnp.imag(b_bar), precision=HIGHEST))
    k_f, k_b = kern[:, 0], kern[:, 1]
    by_lag = jnp.concatenate([k_b[:, :, :0:-1], k_f[:, :, :1] + k_b[:, :, :1], k_f[:, :, 1:],
                              jnp.zeros_like(k_f[:, :, :1])], axis=2)
    skew = jnp.tile(by_lag, (1, 1, tc, 1, 1))[:, :, :tc * (2 * tc - 1)]
    skew = skew.reshape(depth, groups, tc, 2 * tc - 1, s_dim, s_dim)[:, :, :, tc - 1:]
    toeplitz = jnp.transpose(skew, (0, 1, 2, 5, 3, 4)).reshape(depth, groups, tc * s_dim, tc * s_dim)

    f_f = c[:, 0][:, :, None] * powers[:, 0, :, 1:tc + 1, None, :]
    f_b = c[:, 1][:, :, None] * powers[:, 1, :, tc:0:-1, None, :]
    rows = lambda a: jnp.transpose(a, (0, 1, 4, 2, 3)).reshape(depth, groups, p_dim, tc * s_dim)
    w_out = jnp.concatenate([toeplitz, rows(jnp.real(f_f)), rows(jnp.real(f_b)),
                             rows(-jnp.imag(f_f)), rows(-jnp.imag(f_b))], axis=2)

    e_f = powers[:, 0, :, tc - 1::-1][:, :, :tc, :, None] * b_bar[:, 0][:, :, None]
    e_b = powers[:, 1, :, :tc, :, None] * b_bar[:, 1][:, :, None]
    cols = lambda a: jnp.transpose(a, (0, 1, 2, 4, 3)).reshape(depth, groups, tc * s_dim, p_dim)
    w_state = jnp.concatenate([cols(jnp.real(e_f)), cols(jnp.real(e_b)),
                               cols(jnp.imag(e_f)), cols(jnp.imag(e_b))], axis=3)

    last = powers[..., tc, :]
    decay = jnp.stack([jnp.concatenate([jnp.real(last[:, 0]), jnp.real(last[:, 1])], axis=-1),
                       jnp.concatenate([jnp.imag(last[:, 0]), jnp.imag(last[:, 1])], axis=-1)], axis=2)
    return _slab_operators(w_state, w_out, decay, tc, s_dim)


def _slab_operators(w_state, w_out, decay, tc, s_dim):
    depth, groups, cs, st = w_state.shape
    gl = LANES // s_dim
    k = groups // gl
    p_dim = st // 4
    w_state, w_out = lax.optimization_barrier((w_state.astype(BF16), w_out.astype(BF16)))

    def stacked_rows(a, piece):
        rows, cols = a.shape[2:]
        a = a.reshape(depth, k, gl, rows // piece, piece, cols)
        return jnp.transpose(a, (0, 1, 3, 2, 4, 5)).reshape(depth, k, gl * rows, cols).astype(BF16)

    def spread(cols, piece):
        c = np.arange(gl * cols)
        src = c // (gl * piece) * piece + c % piece
        return jnp.asarray(np.arange(cols)[:, None] == src[None, :], dtype=BF16)

    ws = expand_block_diagonal(stacked_rows(w_state, s_dim), spread(st, p_dim), s_dim, p_dim, gl)
    top = expand_block_diagonal(stacked_rows(w_out[:, :, :cs], s_dim), spread(cs, s_dim), s_dim, s_dim, gl)
    bot = expand_block_diagonal(stacked_rows(w_out[:, :, cs:], p_dim), spread(cs, s_dim), p_dim, s_dim, gl)
    decay_s = jnp.transpose(decay.reshape(depth, k, gl, 2, 2, p_dim), (0, 1, 3, 4, 2, 5))
    return ws, top, bot, decay_s.reshape(depth, k, 2, 2 * gl * p_dim)


def _expand_body(a_ref, c_ref, o_ref, *, row_piece, col_piece, gl):
    r = jnp.dot(a_ref[...], c_ref[...], preferred_element_type=F32)
    row0 = pl.program_id(2) * a_ref.shape[0]
    row_group = (row0 + lax.broadcasted_iota(jnp.int32, (a_ref.shape[0], 1), 0)) // row_piece % gl
    col_group = lax.broadcasted_iota(jnp.int32, (1, c_ref.shape[1]), 1) // col_piece % gl
    o_ref[...] = jnp.where(row_group == col_group, r, 0.0).astype(o_ref.dtype)


def expand_block_diagonal(a, spread, row_piece, col_piece, gl):
    depth, k, rows, cols = a.shape
    tr = _tile(rows, 512, 16)
    return pl.pallas_call(
        functools.partial(_expand_body, row_piece=row_piece, col_piece=col_piece, gl=gl),
        grid=(depth, k, rows // tr),
        in_specs=[pl.BlockSpec((None, None, tr, cols), lambda l, i, r: (l, i, r, 0)),
                  pl.BlockSpec(spread.shape, lambda l, i, r: (0, 0))],
        out_specs=pl.BlockSpec((None, None, tr, gl * cols), lambda l, i, r: (l, i, r, 0)),
        out_shape=jax.ShapeDtypeStruct((depth, k, rows, gl * cols), BF16),
        compiler_params=_params("parallel", "parallel", "parallel"),
        name="expand_block_diagonal",
    )(a, spread)


def _s5_state_body(u_ref, w_ref, decay_ref, h_ref, s_scr, *, n, ctx_chunks):
    half = decay_ref.shape[-1]
    q = half // 2
    rows = n // 2 if n % 16 == 0 else n
    for r in range(0, n, rows):
        s_scr[r:r + rows] = jnp.dot(u_ref[r:r + rows], w_ref[...], preferred_element_type=F32)
    parts = ((slice(0, q), slice(half, half + q)), (slice(q, half), slice(half + q, 2 * half)))
    decays = [(decay_ref[0:1, lo:lo + q], decay_ref[1:2, lo:lo + q]) for lo in (0, q)]

    def scan(j, carry):
        jb = jnp.where(j < ctx_chunks, ctx_chunks - 1 - j, n - 1 + ctx_chunks - j)
        out = []
        for row, (re, im), (a_re, a_im), (c_re, c_im) in zip((j, jb), parts, decays, carry):
            s_re, s_im = s_scr[pl.ds(row, 1), re], s_scr[pl.ds(row, 1), im]
            s_scr[pl.ds(row, 1), re] = c_re
            s_scr[pl.ds(row, 1), im] = c_im
            out.append((a_re * c_re - a_im * c_im + s_re, a_re * c_im + a_im * c_re + s_im))
        return tuple(out)

    zero = jnp.zeros((1, q), F32)
    lax.fori_loop(0, n, scan, ((zero, zero), (zero, zero)))
    h_ref[...] = s_scr[...].astype(h_ref.dtype)


def s5_state(u, w_state_s, decay_s, layer, *, ctx_chunks):
    k, n, cs = u.shape
    st = w_state_s.shape[-1]
    return pl.pallas_call(
        functools.partial(_s5_state_body, n=n, ctx_chunks=ctx_chunks),
        grid=(k,),
        in_specs=[
            pl.BlockSpec((None, n, cs), lambda i: (i, 0, 0)),
            pl.BlockSpec((None, None, cs, st), lambda i: (layer, i, 0, 0)),
            pl.BlockSpec((None, None, 2, st // 2), lambda i: (layer, i, 0, 0)),
        ],
        out_specs=pl.BlockSpec((None, n, st), lambda i: (i, 0, 0)),
        out_shape=jax.ShapeDtypeStruct((k, n, st), BF16),
        scratch_shapes=[pltpu.VMEM((n, st), F32)],
        compiler_params=_params("parallel"),
        name="s5_state",
    )(u, w_state_s, decay_s)


def _s5_readout_body(u_ref, h_ref, wu_ref, wh_ref, y_ref):
    y_ref[...] = (jnp.dot(u_ref[...], wu_ref[...], preferred_element_type=F32)
                  + jnp.dot(h_ref[...], wh_ref[...], preferred_element_type=F32))


def s5_readout(u, h, w_in_chunk, w_from_state, layer, *, tn):
    k, n, cs = u.shape
    st = h.shape[-1]
    return pl.pallas_call(
        _s5_readout_body,
        grid=(k, cs // tn),
        in_specs=[
            pl.BlockSpec((None, n, cs), lambda i, j: (i, 0, 0)),
            pl.BlockSpec((None, n, st), lambda i, j: (i, 0, 0)),
            pl.BlockSpec((None, None, cs, tn), lambda i, j: (layer, i, 0, j)),
            pl.BlockSpec((None, None, st, tn), lambda i, j: (layer, i, 0, j)),
        ],
        out_specs=pl.BlockSpec((None, n, tn), lambda i, j: (i, 0, j)),
        out_shape=jax.ShapeDtypeStruct((k, n, cs), F32),
        compiler_params=_params("parallel", "parallel"),
        name="s5_readout",
    )(u, h, w_in_chunk, w_from_state)


def _s5_glu_body(y_ref, u_ref, d_ref, w_ref, o_ref):
    z = jax.nn.gelu(y_ref[...] + d_ref[...] * u_ref[...].astype(F32))
    gate = _sigmoid(jnp.dot(z.astype(BF16), w_ref[...], preferred_element_type=F32))
    o_ref[...] = (z * gate).astype(o_ref.dtype)


def s5_glu(y, p, d, w_glu, layer, *, width, tr):
    t = p.shape[0]
    return pl.pallas_call(
        _s5_glu_body,
        grid=(t // tr,),
        in_specs=[
            pl.BlockSpec((tr, width), lambda i: (i, 0)),
            pl.BlockSpec((tr, width), lambda i: (i, 3)),
            pl.BlockSpec((None, 1, width), lambda i: (layer, 0, 0)),
            pl.BlockSpec((None, width, width), lambda i: (layer, 0, 0)),
        ],
        out_specs=pl.BlockSpec((tr, width), lambda i: (i, 0)),
        out_shape=jax.ShapeDtypeStruct((t, width), BF16),
        compiler_params=_params("parallel"),
        name="s5_glu",
    )(y, p, d.reshape(-1, 1, width), w_glu)


def _merge_body(gl_ref, yc_ref, ys_ref, yp_ref, ya_ref, gu_ref, gb_ref, wc_ref, ws_ref, wp_ref, wa_ref,
                o_ref):
    g_low = gl_ref[...]
    total = None
    for b, (y_ref, w_ref) in enumerate(((yc_ref, wc_ref), (ys_ref, ws_ref), (yp_ref, wp_ref),
                                        (ya_ref, wa_ref))):
        gate = _sigmoid(jnp.dot(g_low, gu_ref[b], preferred_element_type=F32) + gb_ref[b])
        term = gate * jnp.dot(y_ref[...], w_ref[...], preferred_element_type=F32)
        total = term if total is None else total + term
    o_ref[...] = total.astype(o_ref.dtype)


def merge(p, ys, gate_up, gate_bias, w_outs, layer, *, gate_col, tm, tn):
    t = p.shape[0]
    _, nb, rank, d = gate_up.shape
    row = lambda a: pl.BlockSpec((tm, a.shape[1]), lambda i, j: (i, 0))
    col = lambda a: pl.BlockSpec((None, a.shape[1], tn), lambda i, j: (layer, 0, j))
    return pl.pallas_call(
        _merge_body,
        grid=(t // tm, d // tn),
        in_specs=[pl.BlockSpec((tm, rank), lambda i, j: (i, gate_col))] + [row(y) for y in ys]
                 + [pl.BlockSpec((None, nb, rank, tn), lambda i, j: (layer, 0, 0, j)),
                    pl.BlockSpec((None, nb, 1, tn), lambda i, j: (layer, 0, 0, j))] + [col(w) for w in w_outs],
        out_specs=pl.BlockSpec((tm, tn), lambda i, j: (i, j)),
        out_shape=jax.ShapeDtypeStruct((t, d), BF16),
        compiler_params=_params("parallel", "parallel"),
        name="merge",
    )(p, *ys, gate_up, gate_bias.reshape(-1, nb, 1, d), *w_outs)


def _rope_tables(ctx_len, seq_len, dh):
    pairs = dh // 4
    rows = seq_len // GRID_W
    row = jnp.repeat(jnp.arange(rows), GRID_W).astype(F32)
    col = jnp.tile(jnp.arange(GRID_W), rows).astype(F32)
    inv = ROPE_THETA ** (-jnp.arange(pairs, dtype=F32) / pairs)
    ang = jnp.concatenate([row[:, None] * inv, col[:, None] * inv], axis=-1)
    cos = jnp.repeat(jnp.cos(ang), 2, axis=-1)
    sin = jnp.stack([-jnp.sin(ang), jnp.sin(ang)], axis=-1).reshape(seq_len, dh)
    cos = jnp.concatenate([jnp.ones((ctx_len, dh), F32), cos], axis=0)
    sin = jnp.concatenate([jnp.zeros((ctx_len, dh), F32), sin], axis=0)
    return cos, sin


def kernel(x, c, ctx, c_ctx, ada_down, ada_up, ada_bias, norm_gains, w_in, conv_w, s5_lambda_re, s5_lambda_im, s5_log_step, s5_b_re, s5_b_im, s5_c_re, s5_c_im, s5_d, s5_w_glu, pool_w, pool_scale, qk_norm, gate_up, gate_bias, w_out_conv, w_out_s5, w_out_pool, w_out_attn, w_o, ffn_w_gate, ffn_w_up, ffn_w_down):
    batch, seq_len, d = x.shape
    ctx_len = ctx.shape[1]
    depth = w_in.shape[0]
    assert batch == 1 and c.shape[0] == 1
    t = ctx_len + seq_len
    width = conv_w.shape[-1]
    assert s5_d.shape[-1] == width and pool_scale.shape[-1] == width
    dh = qk_norm.shape[-1]
    q_width = w_out_attn.shape[1]
    rank = gate_up.shape[2]
    in_width = w_in.shape[-1]
    kv_width = (in_width - 5 * width - q_width - rank) // 2
    heads, kv_heads = q_width // dh, kv_width // dh
    groups, s_dim = s5_b_re.shape[2], s5_b_re.shape[-1]
    slabs = width // LANES
    assert groups * s_dim == width and LANES % s_dim == 0
    hidden = ffn_w_gate.shape[-1]
    q_col, k_col, v_col = 5 * width, 5 * width + q_width, 5 * width + q_width + kv_width
    gate_col = (v_col + kv_width) // rank
    assert q_col % dh == 0 and (v_col + kv_width) % rank == 0 and width % dh == 0

    tr = _tile(ctx_len, 256, 8)
    tm = _tile(t, 1280, 256)
    tm_half = _tile(t, 640, 128)
    tk_att = _tile(t, 1280, 256)
    tq = _tile(ctx_len, 256, 128)
    ctx_tiles = ctx_len // tr

    w_in_b, w_o_b, w_gate_b, w_up_b, w_down_b = (w.astype(BF16) for w in (w_in, w_o, ffn_w_gate, ffn_w_up,
                                                                         ffn_w_down))
    w_outs_b = [w.astype(BF16) for w in (w_out_conv, w_out_s5, w_out_pool, w_out_attn)]
    gate_up_b, pool_w_b, w_glu_b = gate_up.astype(BF16), pool_w.astype(BF16), s5_w_glu.astype(BF16)
    w_state, w_s5top, w_s5bot, decay = s5_chunk_weights(s5_lambda_re, s5_lambda_im, s5_log_step,
                                               s5_b_re, s5_b_im, s5_c_re, s5_c_im)
    cos, sin = _rope_tables(ctx_len, seq_len, dh)
    qk_gains = qk_norm.reshape(depth, 2, 1, dh)

    cond = jnp.concatenate([c_ctx[None, :], c, jnp.zeros((6, d), F32)], axis=0)
    mods = ada_modulation(cond, ada_down, ada_up, ada_bias)
    mods = mods[:, :2].reshape(depth, 2, N_MOD, d)
    mods = jnp.concatenate([mods, jnp.zeros((depth, 2, 8 - N_MOD, d), F32)], axis=2)

    stream = jnp.concatenate([ctx[0], x[0]], axis=0)
    row_kw = dict(ctx_tiles=ctx_tiles, tr=tr)
    h = modulate(stream, mods, norm_gains, 0, rows=(0, 1, 0), **row_kw)
    n_chunks = t // S5_CHUNK
    for l in range(depth):
        p = matmul(h, w_in_b, l, tm=tm, tn=_tile(in_width, 1024, 256), out_dtype=BF16)

        q_t = qk_prep(p, qk_gains, cos, sin, l, 0, col0=q_col // dh, heads=heads, tr=tk_att,
                      scale=dh ** -0.5 * math.log2(math.e), transpose=True)
        k_n = qk_prep(p, qk_gains, cos, sin, l, 1, col0=k_col // dh, heads=kv_heads, tr=tk_att,
                      scale=1.0, transpose=False)
        v_t = v_prep(p, col0=v_col // dh, heads=kv_heads, dh=dh, tk=tk_att)
        y_attn = flash_attention(q_t, k_n, v_t, tq=tq, ctx_len=ctx_len)

        y_conv, y_pool = conv_pool(p, conv_w, pool_w_b, pool_scale, l, width=width, tr=tr,
                                   ctx_len=ctx_len, seq_len=seq_len)

        u = p[:, 3 * width:4 * width].reshape(n_chunks, S5_CHUNK, slabs, LANES)
        u = jnp.transpose(u, (2, 0, 1, 3)).reshape(slabs, n_chunks, S5_CHUNK * LANES)
        h_s5 = s5_state(u, w_state, decay, l, ctx_chunks=ctx_len // S5_CHUNK)
        y_s5 = s5_readout(u, h_s5, w_s5top, w_s5bot, l, tn=_tile(S5_CHUNK * LANES, 1024, 256))
        y_s5 = jnp.transpose(y_s5.reshape(slabs, n_chunks, S5_CHUNK, LANES), (1, 2, 0, 3)).reshape(t, width)
        y_s5 = s5_glu(y_s5, p, s5_d, w_glu_b, l, width=width, tr=tm_half)

        merged = merge(p, [y_conv, y_s5, y_pool, y_attn], gate_up_b, gate_bias, w_outs_b, l,
                       gate_col=gate_col, tm=tm, tn=_tile(d, 256, 256))
        mix = matmul(merged, w_o_b, l, tm=tm, tn=_tile(d, 1024, 256), out_dtype=BF16)
        stream, h = residual(stream, mix, mods, norm_gains, l, gate_row=2, post_row=1, nxt=(3, 4, 2), **row_kw)

        hid = ffn_up(h, w_gate_b, w_up_b, l, tm=_tile(t, 1664, 128), tn=_tile(hidden, 256, 128))
        ffn = matmul(hid, w_down_b, l, tm=_tile(t, 832, 64), tn=_tile(d, 256, 256), out_dtype=BF16)
        if l + 1 < depth:
            stream, h = residual(stream, ffn, mods, norm_gains, l, gate_row=5, post_row=3, nxt=(0, 1, 0),
                                 next_layer=l + 1, **row_kw)
        else:
            stream, _ = residual(stream, ffn, mods, norm_gains, l, gate_row=5, post_row=3,
                                 first_tile=ctx_tiles, **row_kw)
    return stream[None]
```
